```python
import jax, jax.numpy as jnp
from jax import lax
import numpy as np

D_MODEL = 1024
BATCH = 16
SEQ = 2048
DEPTH = 2

HEAD_DIM = 64
FOX_HEADS = D_MODEL // HEAD_DIM
SWA_Q_HEADS = D_MODEL // HEAD_DIM
SWA_KV_HEADS = max(1, SWA_Q_HEADS // 8)
SWA_GROUP = SWA_Q_HEADS // SWA_KV_HEADS
WINDOW = 128
Q_BLOCK = 128
D_FF = ((8 * D_MODEL // 3 + 127) // 128) * 128
N_MIXERS = 2
RMS_EPS = 1e-6
NEG_INF = -1e30

kernel_name = "fox_swa_sink_alibi_macaron_hybrid"


def rmsnorm(x, g):
    xf = x.astype(jnp.float32)
    y = xf * lax.rsqrt(jnp.mean(xf * xf, axis=-1, keepdims=True) + RMS_EPS)
    return (y * g.astype(jnp.float32)).astype(x.dtype)


def swiglu(h, w_gate, w_up, w_down):
    return (jax.nn.silu(h @ w_gate) * (h @ w_up)) @ w_down


def alibi_slopes(n_heads):
    return jnp.asarray(2.0 ** (-8.0 * np.arange(1, n_heads + 1) / n_heads), dtype=jnp.float32)


def fox_mixer(h, w_in, b_forget, w_out):
    B, S, D = h.shape
    H, hd = FOX_HEADS, HEAD_DIM
    proj = h @ w_in
    q, k, v, f_logit = jnp.split(proj, [H * hd, 2 * H * hd, 3 * H * hd], axis=-1)
    q = q.reshape(B, S, H, hd)
    k = k.reshape(B, S, H, hd)
    v = v.reshape(B, S, H, hd)
    log_f = jax.nn.log_sigmoid((f_logit + b_forget).astype(jnp.float32))
    c = jnp.cumsum(log_f, axis=1).transpose(0, 2, 1)
    scale = hd ** -0.5
    n_blocks = S // Q_BLOCK
    outs = []
    for i in range(n_blocks):
        q0, q1 = i * Q_BLOCK, (i + 1) * Q_BLOCK
        qb, kb, vb = q[:, q0:q1], k[:, :q1], v[:, :q1]
        s = jnp.einsum('bqhd,bkhd->bhqk', qb, kb).astype(jnp.float32) * scale
        s = s + (c[:, :, q0:q1, None] - c[:, :, None, :q1])
        causal = (q0 + jnp.arange(Q_BLOCK))[:, None] >= jnp.arange(q1)[None, :]
        s = jnp.where(causal, s, NEG_INF)
        p = jax.nn.softmax(s, axis=-1).astype(vb.dtype)
        outs.append(jnp.einsum('bhqk,bkhd->bqhd', p, vb))
    o = jnp.concatenate(outs, axis=1).reshape(B, S, H * hd)
    return o @ w_out


def swa_sink_mixer(h, w_in, sinks, w_out):
    B, S, D = h.shape
    Hq, Hkv, G, hd, W = SWA_Q_HEADS, SWA_KV_HEADS, SWA_GROUP, HEAD_DIM, WINDOW
    proj = h @ w_in
    q, k, v = jnp.split(proj, [Hq * hd, Hq * hd + Hkv * hd], axis=-1)
    nb = S // W
    q = q.reshape(B, nb, W, Hkv, G, hd)
    k = k.reshape(B, nb, W, Hkv, hd)
    v = v.reshape(B, nb, W, Hkv, hd)
    pad = jnp.zeros_like(k[:, :1])
    k_band = jnp.concatenate([jnp.concatenate([pad, k[:, :-1]], axis=1), k], axis=2)
    v_band = jnp.concatenate([jnp.concatenate([pad, v[:, :-1]], axis=1), v], axis=2)
    s = jnp.einsum('bnqkgd,bnskd->bnkgqs', q, k_band).astype(jnp.float32) * (hd ** -0.5)
    qi = jnp.arange(W)[:, None]
    kj = jnp.arange(2 * W)[None, :]
    dist = qi + W - kj
    valid = (dist >= 0) & (dist < W)
    blk = jnp.arange(nb)[:, None, None]
    valid = valid[None] & ((blk > 0) | (kj[None] >= W))
    slopes = alibi_slopes(Hq).reshape(Hkv, G)
    s = s - slopes[:, :, None, None] * dist.astype(jnp.float32)[None, None]
    s = jnp.where(valid[None, :, None, None], s, NEG_INF)
    sink = jnp.broadcast_to(sinks.astype(jnp.float32).reshape(1, 1, Hkv, G, 1, 1), s.shape[:-1] + (1,))
    p = jax.nn.softmax(jnp.concatenate([s, sink], axis=-1), axis=-1)[..., :-1]
    o = jnp.einsum('bnkgqs,bnskd->bnqkgd', p.astype(v_band.dtype), v_band).reshape(B, S, Hq * hd)
    return o @ w_out


def setup_inputs(seed: int = 0) -> dict:
    key = jax.random.key(seed)
    ks = iter(jax.random.split(key, 64))
    f32 = jnp.float32

    def nrm(shape, scale):
        return jax.random.normal(next(ks), shape, f32) * scale

    def gain():
        return 1.0 + nrm((D_MODEL,), 0.01)

    def ffn():
        return (gain(), nrm((D_MODEL, D_FF), D_MODEL ** -0.5), nrm((D_MODEL, D_FF), D_MODEL ** -0.5),
                nrm((D_FF, D_MODEL), D_FF ** -0.5))

    inp = {"x": nrm((BATCH, SEQ, D_MODEL), 1.0)}
    g, wg, wu, wd = ffn()
    inp.update(l0_ffn1_norm=g, l0_ffn1_w_gate=wg, l0_ffn1_w_up=wu, l0_ffn1_w_down=wd)
    inp["l0_mix_norm"] = gain()
    w_qkv = nrm((D_MODEL, 3 * FOX_HEADS * HEAD_DIM), D_MODEL ** -0.5)
    w_f = nrm((D_MODEL, FOX_HEADS), 0.1 * D_MODEL ** -0.5)
    inp["l0_fox_w_in"] = jnp.concatenate([w_qkv, w_f], axis=1)
    inp["l0_fox_b_forget"] = 3.0 + nrm((FOX_HEADS,), 0.5)
    inp["l0_fox_w_out"] = nrm((FOX_HEADS * HEAD_DIM, D_MODEL), (FOX_HEADS * HEAD_DIM) ** -0.5)
    g, wg, wu, wd = ffn()
    inp.update(l0_ffn2_norm=g, l0_ffn2_w_gate=wg, l0_ffn2_w_up=wu, l0_ffn2_w_down=wd)
    g, wg, wu, wd = ffn()
    inp.update(l1_ffn1_norm=g, l1_ffn1_w_gate=wg, l1_ffn1_w_up=wu, l1_ffn1_w_down=wd)
    inp["l1_mix_norm"] = gain()
    inp["l1_swa_w_in"] = nrm((D_MODEL, (SWA_Q_HEADS + 2 * SWA_KV_HEADS) * HEAD_DIM), D_MODEL ** -0.5)
    inp["l1_swa_sinks"] = nrm((SWA_Q_HEADS,), 0.5)
    inp["l1_swa_w_out"] = nrm((SWA_Q_HEADS * HEAD_DIM, D_MODEL), (SWA_Q_HEADS * HEAD_DIM) ** -0.5)
    g, wg, wu, wd = ffn()
    inp.update(l1_ffn2_norm=g, l1_ffn2_w_gate=wg, l1_ffn2_w_up=wu, l1_ffn2_w_down=wd)
    inp["final_norm"] = gain()
    return inp


def reference(x,
              l0_ffn1_norm, l0_ffn1_w_gate, l0_ffn1_w_up, l0_ffn1_w_down,
              l0_mix_norm, l0_fox_w_in, l0_fox_b_forget, l0_fox_w_out,
              l0_ffn2_norm, l0_ffn2_w_gate, l0_ffn2_w_up, l0_ffn2_w_down,
              l1_ffn1_norm, l1_ffn1_w_gate, l1_ffn1_w_up, l1_ffn1_w_down,
              l1_mix_norm, l1_swa_w_in, l1_swa_sinks, l1_swa_w_out,
              l1_ffn2_norm, l1_ffn2_w_gate, l1_ffn2_w_up, l1_ffn2_w_down,
              final_norm):
    layers = [
        ((l0_ffn1_norm, l0_ffn1_w_gate, l0_ffn1_w_up, l0_ffn1_w_down),
         (l0_mix_norm, l0_fox_w_in, l0_fox_b_forget, l0_fox_w_out),
         (l0_ffn2_norm, l0_ffn2_w_gate, l0_ffn2_w_up, l0_ffn2_w_down)),
        ((l1_ffn1_norm, l1_ffn1_w_gate, l1_ffn1_w_up, l1_ffn1_w_down),
         (l1_mix_norm, l1_swa_w_in, l1_swa_sinks, l1_swa_w_out),
         (l1_ffn2_norm, l1_ffn2_w_gate, l1_ffn2_w_up, l1_ffn2_w_down)),
    ]
    mixers = (fox_mixer, swa_sink_mixer)
    h = x
    for i in range(DEPTH):
        ffn1, mix, ffn2 = layers[i]
        h = h + 0.5 * swiglu(rmsnorm(h, ffn1[0]), *ffn1[1:])
        h = h + mixers[i % N_MIXERS](rmsnorm(h, mix[0]), *mix[1:])
        h = h + 0.5 * swiglu(rmsnorm(h, ffn2[0]), *ffn2[1:])
    return rmsnorm(h, final_norm)
```

```python
import functools

import jax
import jax.numpy as jnp
import numpy as np
from jax import lax
from jax.experimental import pallas as pl
from jax.experimental.pallas import tpu as pltpu

F32 = jnp.float32
BF16 = jnp.bfloat16

D_MODEL = 1024
HEAD_DIM = 64
N_HEADS = D_MODEL // HEAD_DIM
SWA_KV_HEADS = 2
SWA_GROUP = N_HEADS // SWA_KV_HEADS
WINDOW = 128
RMS_EPS = 1e-6
NEG_INF = -1e30
QK_SCALE = HEAD_DIM ** -0.5

LANES = 128
HEADS_PER_LANE_BLOCK = LANES // HEAD_DIM
VMEM_LIMIT_BYTES = 56 * 1024 * 1024

TOKEN_TILE = 512
FFN_CHUNK = 512
FOX_TILE = 256


def _rmsnorm(x, g):
    ms = jnp.mean(x * x, axis=-1, keepdims=True)
    return x * lax.rsqrt(ms + RMS_EPS) * g


def _resident(shape):
    zeros = (0,) * len(shape)
    return pl.BlockSpec(shape, lambda *_: zeros, pipeline_mode=pl.Buffered(1))


def _params(*semantics):
    return pltpu.CompilerParams(dimension_semantics=semantics,
                                vmem_limit_bytes=VMEM_LIMIT_BYTES)


def _ffn_kernel(h_ref, g_ref, wg_ref, wu_ref, wd_ref, *rest, d_ff, final_norm):
    if final_norm:
        gf_ref, o_ref, a_ref = rest
    else:
        o_ref, a_ref = rest
    x = h_ref[...]
    xn = _rmsnorm(x, g_ref[...]).astype(BF16)
    for c0 in range(0, d_ff, FFN_CHUNK):
        cw = min(FFN_CHUNK, d_ff - c0)
        gate = jnp.dot(xn, wg_ref[:, c0:c0 + cw], preferred_element_type=F32)
        up = jnp.dot(xn, wu_ref[:, c0:c0 + cw], preferred_element_type=F32)
        a_ref[:, c0:c0 + cw] = (gate * jax.nn.sigmoid(gate) * up).astype(BF16)
    y = jnp.dot(a_ref[...], wd_ref[...], preferred_element_type=F32)
    out = x + 0.5 * y
    if final_norm:
        out = _rmsnorm(out, gf_ref[...])
    o_ref[...] = out


def _ffn(h, g, wg, wu, wd, final_g=None):
    t, d = h.shape
    d_ff = wg.shape[1]
    tile = pl.BlockSpec((TOKEN_TILE, d), lambda i: (i, 0))
    in_specs = [tile, _resident((1, d)), _resident((d, d_ff)), _resident((d, d_ff)),
                _resident((d_ff, d))]
    args = [h, g.reshape(1, d), wg.astype(BF16), wu.astype(BF16), wd.astype(BF16)]
    if final_g is not None:
        in_specs.append(_resident((1, d)))
        args.append(final_g.reshape(1, d))
    return pl.pallas_call(
        functools.partial(_ffn_kernel, d_ff=d_ff, final_norm=final_g is not None),
        grid=(t // TOKEN_TILE,),
        in_specs=in_specs,
        out_specs=tile,
        out_shape=jax.ShapeDtypeStruct((t, d), F32),
        scratch_shapes=[pltpu.VMEM((TOKEN_TILE, d_ff), BF16)],
        compiler_params=_params("parallel"),
        name="ffn",
    )(*args)


def _out_proj_kernel(h_ref, o_ref, w_ref, out_ref):
    out_ref[...] = h_ref[...] + jnp.dot(o_ref[...], w_ref[...], preferred_element_type=F32)


def _out_proj(h, o, w):
    t, d = h.shape
    return pl.pallas_call(
        _out_proj_kernel,
        grid=(t // TOKEN_TILE,),
        in_specs=[pl.BlockSpec((TOKEN_TILE, d), lambda i: (i, 0)),
                  pl.BlockSpec((TOKEN_TILE, o.shape[1]), lambda i: (i, 0)),
                  _resident(w.shape)],
        out_specs=pl.BlockSpec((TOKEN_TILE, d), lambda i: (i, 0)),
        out_shape=jax.ShapeDtypeStruct((t, d), F32),
        compiler_params=_params("parallel"),
        name="out_proj",
    )(h, o, w.astype(BF16))


def _split_bf16x3(x):
    hi = x.astype(BF16)
    r = x - hi.astype(F32)
    mid = r.astype(BF16)
    lo = (r - mid.astype(F32)).astype(BF16)
    return hi, mid, lo


def _fox_proj_kernel(h_ref, g_ref, wqkv_ref, wft_ref, bf_ref, q_ref, k_ref, v_ref, c_ref,
                     carry_ref):
    @pl.when(pl.program_id(1) == 0)
    def _():
        carry_ref[...] = jnp.zeros_like(carry_ref)

    tm = h_ref.shape[1]
    xn = _rmsnorm(h_ref[0], g_ref[...]).astype(BF16)
    qkv = jnp.dot(xn, wqkv_ref[...], preferred_element_type=F32)
    q_ref[0] = (qkv[:, :D_MODEL] * QK_SCALE).astype(BF16)
    k_ref[0] = qkv[:, D_MODEL:2 * D_MODEL].astype(BF16)
    v_ref[0] = qkv[:, 2 * D_MODEL:].astype(BF16)

    f_logit = lax.dot_general(wft_ref[...], xn, (((1,), (1,)), ((), ())),
                              preferred_element_type=F32) + bf_ref[...]
    log_f = jnp.minimum(f_logit, 0.0) - jnp.log1p(jnp.exp(-jnp.abs(f_logit)))
    parts = jnp.concatenate(_split_bf16x3(log_f), axis=0)
    src = lax.broadcasted_iota(jnp.int32, (tm, tm), 0)
    dst = lax.broadcasted_iota(jnp.int32, (tm, tm), 1)
    tri = (src <= dst).astype(BF16)
    sums = jnp.dot(parts, tri, preferred_element_type=F32)
    c = sums[:N_HEADS] + sums[N_HEADS:2 * N_HEADS] + sums[2 * N_HEADS:] + carry_ref[...]
    c_ref[0] = c
    carry_ref[...] = c[:, tm - 1:tm]


def _fox_proj(h, g, w_in, b_forget):
    b, s, d = h.shape
    w_qkv = w_in[:, :3 * d].astype(BF16)
    w_ft = w_in[:, 3 * d:].T.astype(BF16)
    tm = TOKEN_TILE
    qkv_spec = pl.BlockSpec((1, tm, d), lambda i, j: (i, j, 0))
    qkv_shape = jax.ShapeDtypeStruct((b, s, d), BF16)
    return pl.pallas_call(
        _fox_proj_kernel,
        grid=(b, s // tm),
        in_specs=[pl.BlockSpec((1, tm, d), lambda i, j: (i, j, 0)),
                  _resident((1, d)), _resident(w_qkv.shape), _resident(w_ft.shape),
                  _resident((N_HEADS, 1))],
        out_specs=[qkv_spec, qkv_spec, qkv_spec,
                   pl.BlockSpec((1, N_HEADS, tm), lambda i, j: (i, 0, j))],
        out_shape=[qkv_shape, qkv_shape, qkv_shape,
                   jax.ShapeDtypeStruct((b, N_HEADS, s), F32)],
        scratch_shapes=[pltpu.VMEM((N_HEADS, 1), F32)],
        compiler_params=_params("parallel", "arbitrary"),
        name="fox_proj",
    )(h, g.reshape(1, d), w_qkv, w_ft, b_forget.reshape(N_HEADS, 1).astype(F32))


def _fox_attn_kernel(q_ref, k_ref, v_ref, c_ref, o_ref):
    seq = q_ref.shape[1]
    t = FOX_TILE
    pair = pl.program_id(1)
    lane = lax.broadcasted_iota(jnp.int32, (t, LANES), 1)
    row = lax.broadcasted_iota(jnp.int32, (t, t), 0)
    col = lax.broadcasted_iota(jnp.int32, (t, t), 1)
    causal = row >= col

    def q_tile(qi, _):
        q0 = pl.multiple_of(qi * t, t)
        q = q_ref[0, pl.ds(q0, t), :]
        outs = []
        for hh in range(HEADS_PER_LANE_BLOCK):
            in_head = (lane >= hh * HEAD_DIM) & (lane < (hh + 1) * HEAD_DIM)
            qm = jnp.where(in_head, q, jnp.zeros_like(q))
            head = pair * HEADS_PER_LANE_BLOCK + hh
            c_t = c_ref[0, pl.ds(head, 1), pl.ds(q0, t)]
            c_t = jnp.broadcast_to(c_t, (LANES, t)).T[:, :1]

            def scores(j):
                k0 = pl.multiple_of(j * t, t)
                k = k_ref[0, pl.ds(k0, t), :]
                s = lax.dot_general(qm, k, (((1,), (1,)), ((), ())),
                                    preferred_element_type=F32)
                c_s = c_ref[0, pl.ds(head, 1), pl.ds(k0, t)]
                return s + (c_t - c_s), k0

            def update(carry, s, k0):
                m, l, acc = carry
                m_new = jnp.maximum(m, jnp.max(s, axis=-1, keepdims=True))
                alpha = jnp.exp(m - m_new)
                p = jnp.exp(s - m_new)
                l = alpha * l + jnp.sum(p, axis=-1, keepdims=True)
                pv = jnp.dot(p.astype(BF16), v_ref[0, pl.ds(k0, t), :],
                             preferred_element_type=F32)
                return m_new, l, alpha * acc + pv

            def below_diagonal(j, carry):
                s, k0 = scores(j)
                return update(carry, s, k0)

            init = (jnp.full((t, 1), NEG_INF, F32), jnp.zeros((t, 1), F32),
                    jnp.zeros((t, LANES), F32))
            carry = lax.fori_loop(0, qi, below_diagonal, init)
            s, k0 = scores(qi)
            _, l, acc = update(carry, jnp.where(causal, s, NEG_INF), k0)
            outs.append((acc / l, in_head))
        o = jnp.zeros((t, LANES), F32)
        for o_h, in_head in outs:
            o = jnp.where(in_head, o_h, o)
        o_ref[0, pl.ds(q0, t), :] = o.astype(BF16)
        return 0

    lax.fori_loop(0, seq // t, q_tile, 0)


def _fox_attn(q, k, v, c):
    b, s, d = q.shape
    blk = pl.BlockSpec((1, s, LANES), lambda i, p: (i, 0, p))
    return pl.pallas_call(
        _fox_attn_kernel,
        grid=(b, d // LANES),
        in_specs=[blk, blk, blk, pl.BlockSpec((1, N_HEADS, s), lambda i, p: (i, 0, 0))],
        out_specs=blk,
        out_shape=jax.ShapeDtypeStruct((b, s, d), BF16),
        compiler_params=_params("parallel", "parallel"),
        name="fox_attn",
    )(q, k, v, c)


def _swa_proj_kernel(h_ref, g_ref, w_ref, q_ref, k_ref, v_ref):
    xn = _rmsnorm(h_ref[...], g_ref[...]).astype(BF16)
    qkv = jnp.dot(xn, w_ref[...], preferred_element_type=F32)
    kv_w = SWA_KV_HEADS * HEAD_DIM
    q_ref[...] = (qkv[:, :D_MODEL] * QK_SCALE).astype(BF16)
    k_ref[...] = qkv[:, D_MODEL:D_MODEL + kv_w].astype(BF16)
    v_ref[...] = qkv[:, D_MODEL + kv_w:].astype(BF16)


def _swa_proj(h, g, w_in):
    t, d = h.shape
    kv_w = SWA_KV_HEADS * HEAD_DIM
    tm = TOKEN_TILE
    return pl.pallas_call(
        _swa_proj_kernel,
        grid=(t // tm,),
        in_specs=[pl.BlockSpec((tm, d), lambda i: (i, 0)), _resident((1, d)),
                  _resident(w_in.shape)],
        out_specs=[pl.BlockSpec((tm, d), lambda i: (i, 0)),
                   pl.BlockSpec((tm, kv_w), lambda i: (i, 0)),
                   pl.BlockSpec((tm, kv_w), lambda i: (i, 0))],
        out_shape=[jax.ShapeDtypeStruct((t, d), BF16),
                   jax.ShapeDtypeStruct((t, kv_w), BF16),
                   jax.ShapeDtypeStruct((t, kv_w), BF16)],
        compiler_params=_params("parallel"),
        name="swa_proj",
    )(h, g.reshape(1, d), w_in.astype(BF16))


def _alibi_slope(head):
    return float(np.float32(2.0 ** (-8.0 * (head + 1) / N_HEADS)))


def _swa_attn_kernel(sinks_ref, q_ref, kp_ref, kc_ref, vp_ref, vc_ref, o_ref, bias_ref):
    w = WINDOW
    blk = pl.program_id(1)

    @pl.when((pl.program_id(0) == 0) & (blk == 0))
    def _():
        qi = lax.broadcasted_iota(jnp.int32, (w, 2 * w), 0)
        kj = lax.broadcasted_iota(jnp.int32, (w, 2 * w), 1)
        dist = qi + w - kj
        valid = (dist >= 0) & (dist < w)
        dist_f = dist.astype(F32)
        for head in range(N_HEADS):
            bias = jnp.where(valid, -_alibi_slope(head) * dist_f, NEG_INF)
            bias_ref[1, head] = bias
            bias_ref[0, head] = jnp.where(kj >= w, bias, NEG_INF)

    table = jnp.minimum(blk, 1)
    q = q_ref[0]
    k_band = jnp.concatenate([kp_ref[0], kc_ref[0]], axis=0)
    v_band = jnp.concatenate([vp_ref[0], vc_ref[0]], axis=0)
    lane = lax.broadcasted_iota(jnp.int32, (w, LANES), 1)

    def to_half(x, src_half, dst_half):
        if src_half != dst_half:
            x = pltpu.roll(x, HEAD_DIM, axis=1)
        return x

    out_heads = []
    for kv in range(SWA_KV_HEADS):
        in_kv = (lane >= kv * HEAD_DIM) & (lane < (kv + 1) * HEAD_DIM)
        q_rows = []
        for g in range(SWA_GROUP):
            head = kv * SWA_GROUP + g
            q_blk = q[:, (head // 2) * LANES:(head // 2 + 1) * LANES]
            q_blk = to_half(q_blk, head % 2, kv)
            q_rows.append(jnp.where(in_kv, q_blk, jnp.zeros_like(q_blk)))
        qx = jnp.concatenate(q_rows, axis=0)
        s_all = lax.dot_general(qx, k_band, (((1,), (1,)), ((), ())),
                                preferred_element_type=F32)
        p_rows, inv_l = [], []
        for g in range(SWA_GROUP):
            head = kv * SWA_GROUP + g
            sink = sinks_ref[head]
            s = s_all[g * w:(g + 1) * w] + bias_ref[table, head]
            m = jnp.maximum(jnp.max(s, axis=-1, keepdims=True), sink)
            p = jnp.exp(s - m)
            l = jnp.sum(p, axis=-1, keepdims=True) + jnp.exp(sink - m)
            p_rows.append(p.astype(BF16))
            inv_l.append(1.0 / l)
        o_all = jnp.dot(jnp.concatenate(p_rows, axis=0), v_band,
                        preferred_element_type=F32)
        for g in range(SWA_GROUP):
            head = kv * SWA_GROUP + g
            o_h = o_all[g * w:(g + 1) * w] * inv_l[g]
            out_heads.append(to_half(o_h, kv, head % 2))
    for pair in range(N_HEADS // 2):
        o_pair = jnp.where(lane < HEAD_DIM, out_heads[2 * pair], out_heads[2 * pair + 1])
        o_ref[0, :, pair * LANES:(pair + 1) * LANES] = o_pair.astype(BF16)


def _swa_attn(q, k, v, sinks):
    b, s, d = q.shape
    w = WINDOW
    kv_w = SWA_KV_HEADS * HEAD_DIM
    prev = pl.BlockSpec((1, w, kv_w), lambda i, n, *_: (i, jnp.maximum(n - 1, 0), 0))
    cur = pl.BlockSpec((1, w, kv_w), lambda i, n, *_: (i, n, 0))
    q_spec = pl.BlockSpec((1, w, d), lambda i, n, *_: (i, n, 0))
    grid_spec = pltpu.PrefetchScalarGridSpec(
        num_scalar_prefetch=1,
        grid=(b, s // w),
        in_specs=[q_spec, prev, cur, prev, cur],
        out_specs=q_spec,
        scratch_shapes=[pltpu.VMEM((2, N_HEADS, w, 2 * w), F32)],
    )
    return pl.pallas_call(
        _swa_attn_kernel,
        grid_spec=grid_spec,
        out_shape=jax.ShapeDtypeStruct((b, s, d), BF16),
        compiler_params=_params("arbitrary", "arbitrary"),
        name="swa_attn",
    )(sinks.astype(F32), q, k, k, v, v)


def kernel(x, l0_ffn1_norm, l0_ffn1_w_gate, l0_ffn1_w_up, l0_ffn1_w_down, l0_mix_norm, l0_fox_w_in, l0_fox_b_forget, l0_fox_w_out, l0_ffn2_norm, l0_ffn2_w_gate, l0_ffn2_w_up, l0_ffn2_w_down, l1_ffn1_norm, l1_ffn1_w_gate, l1_ffn1_w_up, l1_ffn1_w_down, l1_mix_norm, l1_swa_w_in, l1_swa_sinks, l1_swa_w_out, l1_ffn2_norm, l1_ffn2_w_gate, l1_ffn2_w_up, l1_ffn2_w_down, final_norm):
    b, s, d = x.shape
    t = b * s
    h = x.reshape(t, d)

    h = _ffn(h, l0_ffn1_norm, l0_ffn1_w_gate, l0_ffn1_w_up, l0_ffn1_w_down)
    q, k, v, c = _fox_proj(h.reshape(b, s, d), l0_mix_norm, l0_fox_w_in, l0_fox_b_forget)
    o = _fox_attn(q, k, v, c)
    h = _out_proj(h, o.reshape(t, d), l0_fox_w_out)
    h = _ffn(h, l0_ffn2_norm, l0_ffn2_w_gate, l0_ffn2_w_up, l0_ffn2_w_down)

    h = _ffn(h, l1_ffn1_norm, l1_ffn1_w_gate, l1_ffn1_w_up, l1_ffn1_w_down)
    q, k, v = _swa_proj(h, l1_mix_norm, l1_swa_w_in)
    kv_w = SWA_KV_HEADS * HEAD_DIM
    o = _swa_attn(q.reshape(b, s, d), k.reshape(b, s, kv_w), v.reshape(b, s, kv_w),
                  l1_swa_sinks)
    h = _out_proj(h, o.reshape(t, d), l1_swa_w_out)
    h = _ffn(h, l1_ffn2_norm, l1_ffn2_w_gate, l1_ffn2_w_up, l1_ffn2_w_down,
             final_g=final_norm)
    return h.reshape(b, s, d)
```

```python
import functools

import jax
import jax.numpy as jnp
import numpy as np
from jax import lax
from jax.experimental import pallas as pl
from jax.experimental.pallas import tpu as pltpu

F32 = jnp.float32
BF16 = jnp.bfloat16

D_MODEL = 1024
HEAD_DIM = 64
N_HEADS = D_MODEL // HEAD_DIM
SWA_KV_HEADS = 2
SWA_GROUP = N_HEADS // SWA_KV_HEADS
WINDOW = 128
RMS_EPS = 1e-6
NEG_INF = -1e30
QK_SCALE = HEAD_DIM ** -0.5

LANES = 128
HEADS_PER_LANE_BLOCK = LANES // HEAD_DIM
VMEM_LIMIT_BYTES = 56 * 1024 * 1024

TOKEN_TILE = 512
FFN_CHUNK = 512
FOX_TILE = 512


def _rmsnorm(x, g):
    ms = jnp.mean(x * x, axis=-1, keepdims=True)
    return x * lax.rsqrt(ms + RMS_EPS) * g


def _resident(shape):
    zeros = (0,) * len(shape)
    return pl.BlockSpec(shape, lambda *_: zeros, pipeline_mode=pl.Buffered(1))


def _params(*semantics):
    return pltpu.CompilerParams(dimension_semantics=semantics,
                                vmem_limit_bytes=VMEM_LIMIT_BYTES)


def _ffn_kernel(h_ref, g_ref, wg_ref, wu_ref, wd_ref, *rest, d_ff, final_norm):
    if final_norm:
        gf_ref, o_ref, a_ref = rest
    else:
        o_ref, a_ref = rest
    x = h_ref[...]
    xn = _rmsnorm(x, g_ref[...]).astype(BF16)
    for c0 in range(0, d_ff, FFN_CHUNK):
        cw = min(FFN_CHUNK, d_ff - c0)
        gate = jnp.dot(xn, wg_ref[:, c0:c0 + cw], preferred_element_type=F32)
        up = jnp.dot(xn, wu_ref[:, c0:c0 + cw], preferred_element_type=F32)
        a_ref[:, c0:c0 + cw] = (gate * jax.nn.sigmoid(gate) * up).astype(BF16)
    y = jnp.dot(a_ref[...], wd_ref[...], preferred_element_type=F32)
    out = x + 0.5 * y
    if final_norm:
        out = _rmsnorm(out, gf_ref[...])
    o_ref[...] = out


def _ffn(h, g, wg, wu, wd, final_g=None):
    t, d = h.shape
    d_ff = wg.shape[1]
    tile = pl.BlockSpec((TOKEN_TILE, d), lambda i: (i, 0))
    in_specs = [tile, _resident((1, d)), _resident((d, d_ff)), _resident((d, d_ff)),
                _resident((d_ff, d))]
    args = [h, g.reshape(1, d), wg.astype(BF16), wu.astype(BF16), wd.astype(BF16)]
    if final_g is not None:
        in_specs.append(_resident((1, d)))
        args.append(final_g.reshape(1, d))
    return pl.pallas_call(
        functools.partial(_ffn_kernel, d_ff=d_ff, final_norm=final_g is not None),
        grid=(t // TOKEN_TILE,),
        in_specs=in_specs,
        out_specs=tile,
        out_shape=jax.ShapeDtypeStruct((t, d), F32),
        scratch_shapes=[pltpu.VMEM((TOKEN_TILE, d_ff), BF16)],
        compiler_params=_params("parallel"),
        name="ffn",
    )(*args)


def _out_proj_kernel(h_ref, o_ref, w_ref, out_ref):
    out_ref[...] = h_ref[...] + jnp.dot(o_ref[...], w_ref[...], preferred_element_type=F32)


def _out_proj(h, o, w):
    t, d = h.shape
    return pl.pallas_call(
        _out_proj_kernel,
        grid=(t // TOKEN_TILE,),
        in_specs=[pl.BlockSpec((TOKEN_TILE, d), lambda i: (i, 0)),
                  pl.BlockSpec((TOKEN_TILE, o.shape[1]), lambda i: (i, 0)),
                  _resident(w.shape)],
        out_specs=pl.BlockSpec((TOKEN_TILE, d), lambda i: (i, 0)),
        out_shape=jax.ShapeDtypeStruct((t, d), F32),
        compiler_params=_params("parallel"),
        name="out_proj",
    )(h, o, w.astype(BF16))


def _split_bf16x3(x):
    hi = x.astype(BF16)
    r = x - hi.astype(F32)
    mid = r.astype(BF16)
    lo = (r - mid.astype(F32)).astype(BF16)
    return hi, mid, lo


def _fox_proj_kernel(h_ref, g_ref, wqkv_ref, wft_ref, bf_ref, q_ref, k_ref, v_ref, c_ref,
                     carry_ref):
    @pl.when(pl.program_id(1) == 0)
    def _():
        carry_ref[...] = jnp.zeros_like(carry_ref)

    tm = h_ref.shape[1]
    xn = _rmsnorm(h_ref[0], g_ref[...]).astype(BF16)
    qkv = jnp.dot(xn, wqkv_ref[...], preferred_element_type=F32)
    q_ref[0] = (qkv[:, :D_MODEL] * QK_SCALE).astype(BF16)
    k_ref[0] = qkv[:, D_MODEL:2 * D_MODEL].astype(BF16)
    v_ref[0] = qkv[:, 2 * D_MODEL:].astype(BF16)

    f_logit = lax.dot_general(wft_ref[...], xn, (((1,), (1,)), ((), ())),
                              preferred_element_type=F32) + bf_ref[...]
    log_f = jnp.minimum(f_logit, 0.0) - jnp.log1p(jnp.exp(-jnp.abs(f_logit)))
    parts = jnp.concatenate(_split_bf16x3(log_f), axis=0)
    src = lax.broadcasted_iota(jnp.int32, (tm, tm), 0)
    dst = lax.broadcasted_iota(jnp.int32, (tm, tm), 1)
    tri = (src <= dst).astype(BF16)
    sums = jnp.dot(parts, tri, preferred_element_type=F32)
    c = sums[:N_HEADS] + sums[N_HEADS:2 * N_HEADS] + sums[2 * N_HEADS:] + carry_ref[...]
    c_ref[0] = c
    carry_ref[...] = c[:, tm - 1:tm]


def _fox_proj(h, g, w_in, b_forget):
    b, s, d = h.shape
    w_qkv = w_in[:, :3 * d].astype(BF16)
    w_ft = w_in[:, 3 * d:].T.astype(BF16)
    tm = TOKEN_TILE
    qkv_spec = pl.BlockSpec((1, tm, d), lambda i, j: (i, j, 0))
    qkv_shape = jax.ShapeDtypeStruct((b, s, d), BF16)
    return pl.pallas_call(
        _fox_proj_kernel,
        grid=(b, s // tm),
        in_specs=[pl.BlockSpec((1, tm, d), lambda i, j: (i, j, 0)),
                  _resident((1, d)), _resident(w_qkv.shape), _resident(w_ft.shape),
                  _resident((N_HEADS, 1))],
        out_specs=[qkv_spec, qkv_spec, qkv_spec,
                   pl.BlockSpec((1, N_HEADS, tm), lambda i, j: (i, 0, j))],
        out_shape=[qkv_shape, qkv_shape, qkv_shape,
                   jax.ShapeDtypeStruct((b, N_HEADS, s), F32)],
        scratch_shapes=[pltpu.VMEM((N_HEADS, 1), F32)],
        compiler_params=_params("parallel", "arbitrary"),
        name="fox_proj",
    )(h, g.reshape(1, d), w_qkv, w_ft, b_forget.reshape(N_HEADS, 1).astype(F32))


def _fox_attn_kernel(q_ref, k_ref, v_ref, c_ref, o_ref):
    seq = q_ref.shape[1]
    t = FOX_TILE
    pair = pl.program_id(1)
    heads = range(HEADS_PER_LANE_BLOCK)
    lane = lax.broadcasted_iota(jnp.int32, (t, LANES), 1)
    in_head = [(lane >= hh * HEAD_DIM) & (lane < (hh + 1) * HEAD_DIM) for hh in heads]
    row = lax.broadcasted_iota(jnp.int32, (t, t), 0)
    col = lax.broadcasted_iota(jnp.int32, (t, t), 1)
    causal = row >= col

    def q_tile(qi, _):
        q0 = pl.multiple_of(qi * t, t)
        q = q_ref[0, pl.ds(q0, t), :]
        qm = [jnp.where(in_head[hh], q, jnp.zeros_like(q)) for hh in heads]
        c_t = [jnp.broadcast_to(c_ref[0, pl.ds(pair * HEADS_PER_LANE_BLOCK + hh, 1),
                                      pl.ds(q0, t)], (LANES, t)).T[:, :1] for hh in heads]

        def kv_tile(j, carries, on_diagonal):
            k0 = pl.multiple_of(j * t, t)
            k = k_ref[0, pl.ds(k0, t), :]
            v = v_ref[0, pl.ds(k0, t), :]
            out = []
            for hh in heads:
                m, l, acc = carries[hh]
                c_s = c_ref[0, pl.ds(pair * HEADS_PER_LANE_BLOCK + hh, 1), pl.ds(k0, t)]
                s = lax.dot_general(qm[hh], k, (((1,), (1,)), ((), ())),
                                    preferred_element_type=F32) + (c_t[hh] - c_s)
                if on_diagonal:
                    s = jnp.where(causal, s, NEG_INF)
                m_new = jnp.maximum(m, jnp.max(s, axis=-1, keepdims=True))
                alpha = jnp.exp(m - m_new)
                p = jnp.exp(s - m_new)
                l = alpha * l + jnp.sum(p, axis=-1, keepdims=True)
                pv = jnp.dot(p.astype(BF16), v, preferred_element_type=F32)
                out.append((m_new, l, alpha * acc + pv))
            return tuple(out)

        init = tuple((jnp.full((t, 1), NEG_INF, F32), jnp.zeros((t, 1), F32),
                      jnp.zeros((t, LANES), F32)) for _ in heads)
        carries = lax.fori_loop(0, qi, lambda j, c: kv_tile(j, c, False), init)
        carries = kv_tile(qi, carries, True)
        o = jnp.zeros((t, LANES), F32)
        for hh in heads:
            _, l, acc = carries[hh]
            o = jnp.where(in_head[hh], acc / l, o)
        o_ref[0, pl.ds(q0, t), :] = o.astype(BF16)
        return 0

    lax.fori_loop(0, seq // t, q_tile, 0)


def _fox_attn(q, k, v, c):
    b, s, d = q.shape
    blk = pl.BlockSpec((1, s, LANES), lambda i, p: (i, 0, p))
    return pl.pallas_call(
        _fox_attn_kernel,
        grid=(b, d // LANES),
        in_specs=[blk, blk, blk, pl.BlockSpec((1, N_HEADS, s), lambda i, p: (i, 0, 0))],
        out_specs=blk,
        out_shape=jax.ShapeDtypeStruct((b, s, d), BF16),
        compiler_params=_params("parallel", "parallel"),
        name="fox_attn",
    )(q, k, v, c)


def _swa_proj_kernel(h_ref, g_ref, w_ref, q_ref, k_ref, v_ref):
    xn = _rmsnorm(h_ref[...], g_ref[...]).astype(BF16)
    qkv = jnp.dot(xn, w_ref[...], preferred_element_type=F32)
    kv_w = SWA_KV_HEADS * HEAD_DIM
    q_ref[...] = (qkv[:, :D_MODEL] * QK_SCALE).astype(BF16)
    k_ref[...] = qkv[:, D_MODEL:D_MODEL + kv_w].astype(BF16)
    v_ref[...] = qkv[:, D_MODEL + kv_w:].astype(BF16)


def _swa_proj(h, g, w_in):
    t, d = h.shape
    kv_w = SWA_KV_HEADS * HEAD_DIM
    tm = TOKEN_TILE
    return pl.pallas_call(
        _swa_proj_kernel,
        grid=(t // tm,),
        in_specs=[pl.BlockSpec((tm, d), lambda i: (i, 0)), _resident((1, d)),
                  _resident(w_in.shape)],
        out_specs=[pl.BlockSpec((tm, d), lambda i: (i, 0)),
                   pl.BlockSpec((tm, kv_w), lambda i: (i, 0)),
                   pl.BlockSpec((tm, kv_w), lambda i: (i, 0))],
        out_shape=[jax.ShapeDtypeStruct((t, d), BF16),
                   jax.ShapeDtypeStruct((t, kv_w), BF16),
                   jax.ShapeDtypeStruct((t, kv_w), BF16)],
        compiler_params=_params("parallel"),
        name="swa_proj",
    )(h, g.reshape(1, d), w_in.astype(BF16))


def _alibi_slope(head):
    return float(np.float32(2.0 ** (-8.0 * (head + 1) / N_HEADS)))


def _swa_attn_kernel(sinks_ref, q_ref, kp_ref, kc_ref, vp_ref, vc_ref, o_ref, bias_ref):
    w = WINDOW
    blk = pl.program_id(1)

    @pl.when((pl.program_id(0) == 0) & (blk == 0))
    def _():
        qi = lax.broadcasted_iota(jnp.int32, (w, 2 * w), 0)
        kj = lax.broadcasted_iota(jnp.int32, (w, 2 * w), 1)
        dist = qi + w - kj
        valid = (dist >= 0) & (dist < w)
        dist_f = dist.astype(F32)
        for head in range(N_HEADS):
            bias = jnp.where(valid, -_alibi_slope(head) * dist_f, NEG_INF)
            bias_ref[1, head] = bias
            bias_ref[0, head] = jnp.where(kj >= w, bias, NEG_INF)

    table = jnp.minimum(blk, 1)
    q = q_ref[0]
    k_band = jnp.concatenate([kp_ref[0], kc_ref[0]], axis=0)
    v_band = jnp.concatenate([vp_ref[0], vc_ref[0]], axis=0)
    lane = lax.broadcasted_iota(jnp.int32, (w, LANES), 1)

    def to_half(x, src_half, dst_half):
        if src_half != dst_half:
            x = pltpu.roll(x, HEAD_DIM, axis=1)
        return x

    out_heads = []
    for kv in range(SWA_KV_HEADS):
        in_kv = (lane >= kv * HEAD_DIM) & (lane < (kv + 1) * HEAD_DIM)
        q_rows = []
        for g in range(SWA_GROUP):
            head = kv * SWA_GROUP + g
            q_blk = q[:, (head // 2) * LANES:(head // 2 + 1) * LANES]
            q_blk = to_half(q_blk, head % 2, kv)
            q_rows.append(jnp.where(in_kv, q_blk, jnp.zeros_like(q_blk)))
        qx = jnp.concatenate(q_rows, axis=0)
        s_all = lax.dot_general(qx, k_band, (((1,), (1,)), ((), ())),
                                preferred_element_type=F32)
        p_rows, inv_l = [], []
        for g in range(SWA_GROUP):
            head = kv * SWA_GROUP + g
            sink = sinks_ref[head]
            s = s_all[g * w:(g + 1) * w] + bias_ref[table, head]
            m = jnp.maximum(jnp.max(s, axis=-1, keepdims=True), sink)
            p = jnp.exp(s - m)
            l = jnp.sum(p, axis=-1, keepdims=True) + jnp.exp(sink - m)
            p_rows.append(p.astype(BF16))
            inv_l.append(1.0 / l)
        o_all = jnp.dot(jnp.concatenate(p_rows, axis=0), v_band,
                        preferred_element_type=F32)
        for g in range(SWA_GROUP):
            head = kv * SWA_GROUP + g
            o_h = o_all[g * w:(g + 1) * w] * inv_l[g]
            out_heads.append(to_half(o_h, kv, head % 2))
    for pair in range(N_HEADS // 2):
        o_pair = jnp.where(lane < HEAD_DIM, out_heads[2 * pair], out_heads[2 * pair + 1])
        o_ref[0, :, pair * LANES:(pair + 1) * LANES] = o_pair.astype(BF16)


def _swa_attn(q, k, v, sinks):
    b, s, d = q.shape
    w = WINDOW
    kv_w = SWA_KV_HEADS * HEAD_DIM
    prev = pl.BlockSpec((1, w, kv_w), lambda i, n, *_: (i, jnp.maximum(n - 1, 0), 0))
    cur = pl.BlockSpec((1, w, kv_w), lambda i, n, *_: (i, n, 0))
    q_spec = pl.BlockSpec((1, w, d), lambda i, n, *_: (i, n, 0))
    grid_spec = pltpu.PrefetchScalarGridSpec(
        num_scalar_prefetch=1,
        grid=(b, s // w),
        in_specs=[q_spec, prev, cur, prev, cur],
        out_specs=q_spec,
        scratch_shapes=[pltpu.VMEM((2, N_HEADS, w, 2 * w), F32)],
    )
    return pl.pallas_call(
        _swa_attn_kernel,
        grid_spec=grid_spec,
        out_shape=jax.ShapeDtypeStruct((b, s, d), BF16),
        compiler_params=_params("arbitrary", "arbitrary"),
        name="swa_attn",
    )(sinks.astype(F32), q, k, k, v, v)


def kernel(x, l0_ffn1_norm, l0_ffn1_w_gate, l0_ffn1_w_up, l0_ffn1_w_down, l0_mix_norm, l0_fox_w_in, l0_fox_b_forget, l0_fox_w_out, l0_ffn2_norm, l0_ffn2_w_gate, l0_ffn2_w_up, l0_ffn2_w_down, l1_ffn1_norm, l1_ffn1_w_gate, l1_ffn1_w_up, l1_ffn1_w_down, l1_mix_norm, l1_swa_w_in, l1_swa_sinks, l1_swa_w_out, l1_ffn2_norm, l1_ffn2_w_gate, l1_ffn2_w_up, l1_ffn2_w_down, final_norm):
    b, s, d = x.shape
    t = b * s
    h = x.reshape(t, d)

    h = _ffn(h, l0_ffn1_norm, l0_ffn1_w_gate, l0_ffn1_w_up, l0_ffn1_w_down)
    q, k, v, c = _fox_proj(h.reshape(b, s, d), l0_mix_norm, l0_fox_w_in, l0_fox_b_forget)
    o = _fox_attn(q, k, v, c)
    h = _out_proj(h, o.reshape(t, d), l0_fox_w_out)
    h = _ffn(h, l0_ffn2_norm, l0_ffn2_w_gate, l0_ffn2_w_up, l0_ffn2_w_down)

    h = _ffn(h, l1_ffn1_norm, l1_ffn1_w_gate, l1_ffn1_w_up, l1_ffn1_w_down)
    q, k, v = _swa_proj(h, l1_mix_norm, l1_swa_w_in)
    kv_w = SWA_KV_HEADS * HEAD_DIM
    o = _swa_attn(q.reshape(b, s, d), k.reshape(b, s, kv_w), v.reshape(b, s, kv_w),
                  l1_swa_sinks)
    h = _out_proj(h, o.reshape(t, d), l1_swa_w_out)
    h = _ffn(h, l1_ffn2_norm, l1_ffn2_w_gate, l1_ffn2_w_up, l1_ffn2_w_down,
             final_g=final_norm)
    return h.reshape(b, s, d)
```

```python
import functools

import jax
import jax.numpy as jnp
import numpy as np
from jax import lax
from jax.experimental import pallas as pl
from jax.experimental.pallas import tpu as pltpu

F32 = jnp.float32
BF16 = jnp.bfloat16

D_MODEL = 1024
HEAD_DIM = 64
N_HEADS = D_MODEL // HEAD_DIM
SWA_KV_HEADS = 2
SWA_GROUP = N_HEADS // SWA_KV_HEADS
WINDOW = 128
RMS_EPS = 1e-6
NEG_INF = -1e30
QK_SCALE = HEAD_DIM ** -0.5

LANES = 128
HEADS_PER_LANE_BLOCK = LANES // HEAD_DIM
VMEM_LIMIT_BYTES = 56 * 1024 * 1024

TOKEN_TILE = 512
FFN_CHUNK = 512
FOX_TILE = 512


def _rmsnorm(x, g):
    ms = jnp.mean(x * x, axis=-1, keepdims=True)
    return x * lax.rsqrt(ms + RMS_EPS) * g


def _resident(shape):
    zeros = (0,) * len(shape)
    return pl.BlockSpec(shape, lambda *_: zeros, pipeline_mode=pl.Buffered(1))


def _params(*semantics):
    return pltpu.CompilerParams(dimension_semantics=semantics,
                                vmem_limit_bytes=VMEM_LIMIT_BYTES)


def _ffn_kernel(h_ref, g_ref, wg_ref, wu_ref, wd_ref, *rest, d_ff, final_norm):
    if final_norm:
        gf_ref, o_ref, a_ref = rest
    else:
        o_ref, a_ref = rest
    x = h_ref[...]
    xn = _rmsnorm(x, g_ref[...]).astype(BF16)
    for c0 in range(0, d_ff, FFN_CHUNK):
        cw = min(FFN_CHUNK, d_ff - c0)
        gate = jnp.dot(xn, wg_ref[:, c0:c0 + cw], preferred_element_type=F32)
        up = jnp.dot(xn, wu_ref[:, c0:c0 + cw], preferred_element_type=F32)
        a_ref[:, c0:c0 + cw] = (gate * jax.nn.sigmoid(gate) * up).astype(BF16)
    y = jnp.dot(a_ref[...], wd_ref[...], preferred_element_type=F32)
    out = x + 0.5 * y
    if final_norm:
        out = _rmsnorm(out, gf_ref[...])
    o_ref[...] = out


def _ffn(h, g, wg, wu, wd, final_g=None):
    t, d = h.shape
    d_ff = wg.shape[1]
    tile = pl.BlockSpec((TOKEN_TILE, d), lambda i: (i, 0))
    in_specs = [tile, _resident((1, d)), _resident((d, d_ff)), _resident((d, d_ff)),
                _resident((d_ff, d))]
    args = [h, g.reshape(1, d), wg.astype(BF16), wu.astype(BF16), wd.astype(BF16)]
    if final_g is not None:
        in_specs.append(_resident((1, d)))
        args.append(final_g.reshape(1, d))
    return pl.pallas_call(
        functools.partial(_ffn_kernel, d_ff=d_ff, final_norm=final_g is not None),
        grid=(t // TOKEN_TILE,),
        in_specs=in_specs,
        out_specs=tile,
        out_shape=jax.ShapeDtypeStruct((t, d), F32),
        scratch_shapes=[pltpu.VMEM((TOKEN_TILE, d_ff), BF16)],
        compiler_params=_params("parallel"),
        name="ffn",
    )(*args)


def _out_proj_kernel(h_ref, o_ref, w_ref, out_ref):
    out_ref[...] = h_ref[...] + jnp.dot(o_ref[...], w_ref[...], preferred_element_type=F32)


def _out_proj(h, o, w):
    t, d = h.shape
    return pl.pallas_call(
        _out_proj_kernel,
        grid=(t // TOKEN_TILE,),
        in_specs=[pl.BlockSpec((TOKEN_TILE, d), lambda i: (i, 0)),
                  pl.BlockSpec((TOKEN_TILE, o.shape[1]), lambda i: (i, 0)),
                  _resident(w.shape)],
        out_specs=pl.BlockSpec((TOKEN_TILE, d), lambda i: (i, 0)),
        out_shape=jax.ShapeDtypeStruct((t, d), F32),
        compiler_params=_params("parallel"),
        name="out_proj",
    )(h, o, w.astype(BF16))


def _split_bf16x3(x):
    hi = x.astype(BF16)
    r = x - hi.astype(F32)
    mid = r.astype(BF16)
    lo = (r - mid.astype(F32)).astype(BF16)
    return hi, mid, lo


AUG_PARTS = 3
LOG2E = float(np.log2(np.e))


def _data_half(head):
    return head % HEADS_PER_LANE_BLOCK


def _aug_base(head):
    return HEAD_DIM * (1 - _data_half(head))


def _fox_aug_tables():
    width = N_HEADS * LANES
    place = np.zeros((LANES, 2 * width), np.float32)
    const = np.zeros((1, 2 * width), np.float32)
    v_const = np.zeros((HEADS_PER_LANE_BLOCK, LANES), np.float32)
    for head in range(N_HEADS):
        base = head * LANES + _aug_base(head)
        for part in range(AUG_PARTS):
            place[part * N_HEADS + head, base + part] = 1.0
            place[part * N_HEADS + head, width + base + AUG_PARTS + part] = -1.0
            const[0, base + AUG_PARTS + part] = 1.0
            const[0, width + base + part] = 1.0
    for half in range(HEADS_PER_LANE_BLOCK):
        v_const[half, HEAD_DIM * (1 - half)] = 1.0
    return jnp.asarray(place, BF16), jnp.asarray(const, F32), jnp.asarray(v_const, F32)


def _fox_proj_kernel(h_ref, g_ref, wqkv_ref, wf_ref, bf_ref, place_ref, const_ref, vconst_ref,
                     q_ref, k_ref, v_ref, carry_ref):
    @pl.when(pl.program_id(1) == 0)
    def _():
        carry_ref[...] = jnp.zeros_like(carry_ref)

    tm = h_ref.shape[1]
    width = N_HEADS * LANES
    xn = _rmsnorm(h_ref[0], g_ref[...]).astype(BF16)
    qkv = jnp.dot(xn, wqkv_ref[...], preferred_element_type=F32)

    lane = lax.broadcasted_iota(jnp.int32, (tm, LANES), 1)

    def by_part(parts):
        out = jnp.zeros((tm, LANES), F32)
        for p in reversed(range(AUG_PARTS)):
            out = jnp.where(lane < (p + 1) * N_HEADS, parts[p], out)
        return out.astype(BF16)

    f_logit = jnp.dot(xn, wf_ref[...], preferred_element_type=F32) + bf_ref[...]
    log_f = jnp.minimum(f_logit, 0.0) - jnp.log1p(jnp.exp(-jnp.abs(f_logit)))
    parts = by_part([p.astype(F32) for p in _split_bf16x3(log_f)])
    t_out = lax.broadcasted_iota(jnp.int32, (tm, tm), 0)
    t_in = lax.broadcasted_iota(jnp.int32, (tm, tm), 1)
    tri = (t_in <= t_out).astype(BF16)
    sums = jnp.dot(tri, parts, preferred_element_type=F32)
    c = sums + carry_ref[...]
    for p in range(1, AUG_PARTS):
        c = c + pltpu.roll(sums, LANES - p * N_HEADS, axis=1)
    carry_ref[...] = c[tm - 1:tm, :]

    c_parts = [p.astype(F32) for p in _split_bf16x3(c * LOG2E)]
    c_parts = by_part([c_parts[0]] + [pltpu.roll(c_parts[p], p * N_HEADS, axis=1)
                                      for p in range(1, AUG_PARTS)])
    aug = jnp.dot(c_parts, place_ref[...], preferred_element_type=F32) + const_ref[...]

    for head in range(N_HEADS):
        half = _data_half(head)
        is_data = (lane >= half * HEAD_DIM) & (lane < (half + 1) * HEAD_DIM)
        src = (head // HEADS_PER_LANE_BLOCK) * LANES
        dst = slice(head * LANES, (head + 1) * LANES)
        q = qkv[:, src:src + LANES] * (QK_SCALE * LOG2E)
        k = qkv[:, D_MODEL + src:D_MODEL + src + LANES]
        v = qkv[:, 2 * D_MODEL + src:2 * D_MODEL + src + LANES]
        q_ref[0, :, dst] = jnp.where(is_data, q, aug[:, dst]).astype(BF16)
        k_ref[0, :, dst] = jnp.where(is_data, k, aug[:, width + head * LANES:
                                                      width + (head + 1) * LANES]).astype(BF16)
        v_ref[0, :, dst] = jnp.where(is_data, v, vconst_ref[half:half + 1, :]).astype(BF16)


def _fox_proj(h, g, w_in, b_forget):
    b, s, d = h.shape
    w_qkv = w_in[:, :3 * d].astype(BF16)
    pad = LANES - AUG_PARTS * N_HEADS
    w_f = jnp.pad(jnp.tile(w_in[:, 3 * d:], (1, AUG_PARTS)), ((0, 0), (0, pad))).astype(BF16)
    b_f = jnp.pad(jnp.tile(b_forget.astype(F32), AUG_PARTS), (0, pad)).reshape(1, LANES)
    place, const, v_const = _fox_aug_tables()
    tm = TOKEN_TILE
    out_spec = pl.BlockSpec((1, tm, N_HEADS * LANES), lambda i, j: (i, j, 0))
    out_shape = jax.ShapeDtypeStruct((b, s, N_HEADS * LANES), BF16)
    return pl.pallas_call(
        _fox_proj_kernel,
        grid=(b, s // tm),
        in_specs=[pl.BlockSpec((1, tm, d), lambda i, j: (i, j, 0)),
                  _resident((1, d)), _resident(w_qkv.shape), _resident(w_f.shape),
                  _resident(b_f.shape), _resident(place.shape), _resident(const.shape),
                  _resident(v_const.shape)],
        out_specs=[out_spec, out_spec, out_spec],
        out_shape=[out_shape, out_shape, out_shape],
        scratch_shapes=[pltpu.VMEM((1, LANES), F32)],
        compiler_params=_params("parallel", "arbitrary"),
        name="fox_proj",
    )(h, g.reshape(1, d), w_qkv, w_f, b_f, place, const, v_const)


def _fox_attn_kernel(q_ref, k_ref, v_ref, o_ref):
    seq = q_ref.shape[1]
    t = FOX_TILE
    half = t // 2
    heads = range(HEADS_PER_LANE_BLOCK)
    lane = lax.broadcasted_iota(jnp.int32, (half, LANES), 1)
    upper_mask = (lax.broadcasted_iota(jnp.int32, (half, half), 0)
                  >= lax.broadcasted_iota(jnp.int32, (half, half), 1))
    lower_mask = (lax.broadcasted_iota(jnp.int32, (half, t), 0) + half
                  >= lax.broadcasted_iota(jnp.int32, (half, t), 1))

    def update(m, acc, q, k, v, mask=None):
        s = lax.dot_general(q, k, (((1,), (1,)), ((), ())), preferred_element_type=F32)
        if mask is not None:
            s = jnp.where(mask, s, NEG_INF)
        m_new = jnp.maximum(m, jnp.max(s, axis=-1, keepdims=True))
        p = jnp.exp2(s - m_new).astype(BF16)
        return m_new, jnp.exp2(m - m_new) * acc + jnp.dot(p, v, preferred_element_type=F32)

    for qi in range(seq // t):
        q0 = qi * t
        done = []
        for hh in heads:
            cols = slice(hh * LANES, (hh + 1) * LANES)
            q = q_ref[0, q0:q0 + t, cols]
            m = jnp.full((t, 1), NEG_INF, F32)
            acc = jnp.zeros((t, LANES), F32)
            for j in range(qi):
                m, acc = update(m, acc, q, k_ref[0, j * t:(j + 1) * t, cols],
                                v_ref[0, j * t:(j + 1) * t, cols])
            k = k_ref[0, q0:q0 + t, cols]
            v = v_ref[0, q0:q0 + t, cols]
            _, acc_u = update(m[:half], acc[:half], q[:half], k[:half], v[:half], upper_mask)
            _, acc_l = update(m[half:], acc[half:], q[half:], k, v, lower_mask)
            base = _aug_base(hh)
            done.append([a / a[:, base:base + 1] for a in (acc_u, acc_l)])
        for part in range(2):
            r0 = q0 + part * half
            o_ref[0, r0:r0 + half, :] = jnp.where(lane < HEAD_DIM, done[0][part],
                                                  done[1][part]).astype(BF16)


def _fox_attn(q, k, v):
    b, s, _ = q.shape
    pair_w = HEADS_PER_LANE_BLOCK * LANES
    blk = pl.BlockSpec((1, s, pair_w), lambda i, p: (i, 0, p))
    return pl.pallas_call(
        _fox_attn_kernel,
        grid=(b, N_HEADS // HEADS_PER_LANE_BLOCK),
        in_specs=[blk, blk, blk],
        out_specs=pl.BlockSpec((1, s, LANES), lambda i, p: (i, 0, p)),
        out_shape=jax.ShapeDtypeStruct((b, s, D_MODEL), BF16),
        compiler_params=_params("parallel", "parallel"),
        name="fox_attn",
    )(q, k, v)


def _swa_proj_kernel(h_ref, g_ref, w_ref, q_ref, k_ref, v_ref):
    xn = _rmsnorm(h_ref[...], g_ref[...]).astype(BF16)
    qkv = jnp.dot(xn, w_ref[...], preferred_element_type=F32)
    kv_w = SWA_KV_HEADS * HEAD_DIM
    q_ref[...] = (qkv[:, :D_MODEL] * QK_SCALE).astype(BF16)
    k_ref[...] = qkv[:, D_MODEL:D_MODEL + kv_w].astype(BF16)
    v_ref[...] = qkv[:, D_MODEL + kv_w:].astype(BF16)


def _swa_proj(h, g, w_in):
    t, d = h.shape
    kv_w = SWA_KV_HEADS * HEAD_DIM
    tm = TOKEN_TILE
    return pl.pallas_call(
        _swa_proj_kernel,
        grid=(t // tm,),
        in_specs=[pl.BlockSpec((tm, d), lambda i: (i, 0)), _resident((1, d)),
                  _resident(w_in.shape)],
        out_specs=[pl.BlockSpec((tm, d), lambda i: (i, 0)),
                   pl.BlockSpec((tm, kv_w), lambda i: (i, 0)),
                   pl.BlockSpec((tm, kv_w), lambda i: (i, 0))],
        out_shape=[jax.ShapeDtypeStruct((t, d), BF16),
                   jax.ShapeDtypeStruct((t, kv_w), BF16),
                   jax.ShapeDtypeStruct((t, kv_w), BF16)],
        compiler_params=_params("parallel"),
        name="swa_proj",
    )(h, g.reshape(1, d), w_in.astype(BF16))


def _alibi_slope(head):
    return float(np.float32(2.0 ** (-8.0 * (head + 1) / N_HEADS)))


def _swa_attn_kernel(sinks_ref, q_ref, kp_ref, kc_ref, vp_ref, vc_ref, o_ref, bias_ref):
    w = WINDOW
    blk = pl.program_id(1)

    @pl.when((pl.program_id(0) == 0) & (blk == 0))
    def _():
        qi = lax.broadcasted_iota(jnp.int32, (w, 2 * w), 0)
        kj = lax.broadcasted_iota(jnp.int32, (w, 2 * w), 1)
        dist = qi + w - kj
        valid = (dist >= 0) & (dist < w)
        dist_f = dist.astype(F32)
        for head in range(N_HEADS):
            bias = jnp.where(valid, -_alibi_slope(head) * dist_f, NEG_INF)
            bias_ref[1, head] = bias
            bias_ref[0, head] = jnp.where(kj >= w, bias, NEG_INF)

    table = jnp.minimum(blk, 1)
    q = q_ref[0]
    k_band = jnp.concatenate([kp_ref[0], kc_ref[0]], axis=0)
    v_band = jnp.concatenate([vp_ref[0], vc_ref[0]], axis=0)
    lane = lax.broadcasted_iota(jnp.int32, (w, LANES), 1)

    def to_half(x, src_half, dst_half):
        if src_half != dst_half:
            x = pltpu.roll(x, HEAD_DIM, axis=1)
        return x

    out_heads = []
    for kv in range(SWA_KV_HEADS):
        in_kv = (lane >= kv * HEAD_DIM) & (lane < (kv + 1) * HEAD_DIM)
        q_rows = []
        for g in range(SWA_GROUP):
            head = kv * SWA_GROUP + g
            q_blk = q[:, (head // 2) * LANES:(head // 2 + 1) * LANES]
            q_blk = to_half(q_blk, head % 2, kv)
            q_rows.append(jnp.where(in_kv, q_blk, jnp.zeros_like(q_blk)))
        qx = jnp.concatenate(q_rows, axis=0)
        s_all = lax.dot_general(qx, k_band, (((1,), (1,)), ((), ())),
                                preferred_element_type=F32)
        p_rows, inv_l = [], []
        for g in range(SWA_GROUP):
            head = kv * SWA_GROUP + g
            sink = sinks_ref[head]
            s = s_all[g * w:(g + 1) * w] + bias_ref[table, head]
            m = jnp.maximum(jnp.max(s, axis=-1, keepdims=True), sink)
            p = jnp.exp(s - m)
            l = jnp.sum(p, axis=-1, keepdims=True) + jnp.exp(sink - m)
            p_rows.append(p.astype(BF16))
            inv_l.append(1.0 / l)
        o_all = jnp.dot(jnp.concatenate(p_rows, axis=0), v_band,
                        preferred_element_type=F32)
        for g in range(SWA_GROUP):
            head = kv * SWA_GROUP + g
            o_h = o_all[g * w:(g + 1) * w] * inv_l[g]
            out_heads.append(to_half(o_h, kv, head % 2))
    for pair in range(N_HEADS // 2):
        o_pair = jnp.where(lane < HEAD_DIM, out_heads[2 * pair], out_heads[2 * pair + 1])
        o_ref[0, :, pair * LANES:(pair + 1) * LANES] = o_pair.astype(BF16)


def _swa_attn(q, k, v, sinks):
    b, s, d = q.shape
    w = WINDOW
    kv_w = SWA_KV_HEADS * HEAD_DIM
    prev = pl.BlockSpec((1, w, kv_w), lambda i, n, *_: (i, jnp.maximum(n - 1, 0), 0))
    cur = pl.BlockSpec((1, w, kv_w), lambda i, n, *_: (i, n, 0))
    q_spec = pl.BlockSpec((1, w, d), lambda i, n, *_: (i, n, 0))
    grid_spec = pltpu.PrefetchScalarGridSpec(
        num_scalar_prefetch=1,
        grid=(b, s // w),
        in_specs=[q_spec, prev, cur, prev, cur],
        out_specs=q_spec,
        scratch_shapes=[pltpu.VMEM((2, N_HEADS, w, 2 * w), F32)],
    )
    return pl.pallas_call(
        _swa_attn_kernel,
        grid_spec=grid_spec,
        out_shape=jax.ShapeDtypeStruct((b, s, d), BF16),
        compiler_params=_params("arbitrary", "arbitrary"),
        name="swa_attn",
    )(sinks.astype(F32), q, k, k, v, v)


def kernel(x, l0_ffn1_norm, l0_ffn1_w_gate, l0_ffn1_w_up, l0_ffn1_w_down, l0_mix_norm, l0_fox_w_in, l0_fox_b_forget, l0_fox_w_out, l0_ffn2_norm, l0_ffn2_w_gate, l0_ffn2_w_up, l0_ffn2_w_down, l1_ffn1_norm, l1_ffn1_w_gate, l1_ffn1_w_up, l1_ffn1_w_down, l1_mix_norm, l1_swa_w_in, l1_swa_sinks, l1_swa_w_out, l1_ffn2_norm, l1_ffn2_w_gate, l1_ffn2_w_up, l1_ffn2_w_down, final_norm):
    b, s, d = x.shape
    t = b * s
    h = x.reshape(t, d)

    h = _ffn(h, l0_ffn1_norm, l0_ffn1_w_gate, l0_ffn1_w_up, l0_ffn1_w_down)
    q, k, v = _fox_proj(h.reshape(b, s, d), l0_mix_norm, l0_fox_w_in, l0_fox_b_forget)
    o = _fox_attn(q, k, v)
    h = _out_proj(h, o.reshape(t, d), l0_fox_w_out)
    h = _ffn(h, l0_ffn2_norm, l0_ffn2_w_gate, l0_ffn2_w_up, l0_ffn2_w_down)

    h = _ffn(h, l1_ffn1_norm, l1_ffn1_w_gate, l1_ffn1_w_up, l1_ffn1_w_down)
    q, k, v = _swa_proj(h, l1_mix_norm, l1_swa_w_in)
    kv_w = SWA_KV_HEADS * HEAD_DIM
    o = _swa_attn(q.reshape(b, s, d), k.reshape(b, s, kv_w), v.reshape(b, s, kv_w),
                  l1_swa_sinks)
    h = _out_proj(h, o.reshape(t, d), l1_swa_w_out)
    h = _ffn(h, l1_ffn2_norm, l1_ffn2_w_gate, l1_ffn2_w_up, l1_ffn2_w_down,
             final_g=final_norm)
    return h.reshape(b, s, d)
```

```python
import functools

import jax
import jax.numpy as jnp
import numpy as np
from jax import lax
from jax.experimental import pallas as pl
from jax.experimental.pallas import tpu as pltpu

F32 = jnp.float32
BF16 = jnp.bfloat16

D_MODEL = 1024
HEAD_DIM = 64
N_HEADS = D_MODEL // HEAD_DIM
SWA_KV_HEADS = 2
SWA_GROUP = N_HEADS // SWA_KV_HEADS
WINDOW = 128
RMS_EPS = 1e-6
NEG_INF = -1e30
QK_SCALE = HEAD_DIM ** -0.5

LANES = 128
HEADS_PER_LANE_BLOCK = LANES // HEAD_DIM
VMEM_LIMIT_BYTES = 56 * 1024 * 1024

TOKEN_TILE = 512
FFN_CHUNK = 512
FOX_TILE = 512
FOX_SCORES_AHEAD = 2
SWA_BLOCKS_PER_STEP = 4
SWA_HEADS_PER_JOB = 4
SWA_SCORES_AHEAD = 1


def _rmsnorm(x, g):
    ms = jnp.mean(x * x, axis=-1, keepdims=True)
    return x * lax.rsqrt(ms + RMS_EPS) * g


def _resident(shape):
    zeros = (0,) * len(shape)
    return pl.BlockSpec(shape, lambda *_: zeros, pipeline_mode=pl.Buffered(1))


def _params(*semantics):
    return pltpu.CompilerParams(dimension_semantics=semantics,
                                vmem_limit_bytes=VMEM_LIMIT_BYTES)


def _ffn_kernel(*refs, d_ff, mixer_out, final_norm):
    refs = list(refs)
    h_ref = refs.pop(0)
    attn_ref, wo_ref = (refs.pop(0), refs.pop(0)) if mixer_out else (None, None)
    g_ref, wg_ref, wu_ref, wd_ref = (refs.pop(0) for _ in range(4))
    gf_ref = refs.pop(0) if final_norm else None
    o_ref, a_ref = refs
    x = h_ref[...]
    if mixer_out:
        x = x + jnp.dot(attn_ref[...], wo_ref[...], preferred_element_type=F32)
    xn = _rmsnorm(x, g_ref[...]).astype(BF16)
    for c0 in range(0, d_ff, FFN_CHUNK):
        cw = min(FFN_CHUNK, d_ff - c0)
        gate = jnp.dot(xn, wg_ref[:, c0:c0 + cw], preferred_element_type=F32)
        up = jnp.dot(xn, wu_ref[:, c0:c0 + cw], preferred_element_type=F32)
        a_ref[:, c0:c0 + cw] = (gate * jax.nn.sigmoid(gate) * up).astype(BF16)
    y = jnp.dot(a_ref[...], wd_ref[...], preferred_element_type=F32)
    out = x + 0.5 * y
    if final_norm:
        out = _rmsnorm(out, gf_ref[...])
    o_ref[...] = out


def _ffn(h, g, wg, wu, wd, attn=None, w_out=None, final_g=None):
    t, d = h.shape
    d_ff = wg.shape[1]
    tile = pl.BlockSpec((TOKEN_TILE, d), lambda i: (i, 0))
    in_specs, args = [tile], [h]
    if attn is not None:
        in_specs += [pl.BlockSpec((TOKEN_TILE, attn.shape[1]), lambda i: (i, 0)),
                     _resident(w_out.shape)]
        args += [attn, w_out.astype(BF16)]
    in_specs += [_resident((1, d)), _resident((d, d_ff)), _resident((d, d_ff)),
                 _resident((d_ff, d))]
    args += [g.reshape(1, d), wg.astype(BF16), wu.astype(BF16), wd.astype(BF16)]
    if final_g is not None:
        in_specs.append(_resident((1, d)))
        args.append(final_g.reshape(1, d))
    return pl.pallas_call(
        functools.partial(_ffn_kernel, d_ff=d_ff, mixer_out=attn is not None,
                          final_norm=final_g is not None),
        grid=(t // TOKEN_TILE,),
        in_specs=in_specs,
        out_specs=tile,
        out_shape=jax.ShapeDtypeStruct((t, d), F32),
        scratch_shapes=[pltpu.VMEM((TOKEN_TILE, d_ff), BF16)],
        compiler_params=_params("parallel"),
        name="ffn",
    )(*args)


def _split_bf16x3(x):
    hi = x.astype(BF16)
    r = x - hi.astype(F32)
    mid = r.astype(BF16)
    lo = (r - mid.astype(F32)).astype(BF16)
    return hi, mid, lo


AUG_PARTS = 3
LOG2E = float(np.log2(np.e))


def _data_half(head):
    return head % HEADS_PER_LANE_BLOCK


def _aug_base(head):
    return HEAD_DIM * (1 - _data_half(head))


def _fox_aug_tables():
    width = N_HEADS * LANES
    place = np.zeros((LANES, 2 * width), np.float32)
    const = np.zeros((1, 2 * width), np.float32)
    v_const = np.zeros((HEADS_PER_LANE_BLOCK, LANES), np.float32)
    for head in range(N_HEADS):
        base = head * LANES + _aug_base(head)
        for part in range(AUG_PARTS):
            place[part * N_HEADS + head, base + part] = 1.0
            place[part * N_HEADS + head, width + base + AUG_PARTS + part] = -1.0
            const[0, base + AUG_PARTS + part] = 1.0
            const[0, width + base + part] = 1.0
    for half in range(HEADS_PER_LANE_BLOCK):
        v_const[half, HEAD_DIM * (1 - half)] = 1.0
    return jnp.asarray(place, BF16), jnp.asarray(const, F32), jnp.asarray(v_const, F32)


def _fox_proj_kernel(h_ref, g_ref, wqkv_ref, wf_ref, bf_ref, place_ref, const_ref, vconst_ref,
                     q_ref, k_ref, v_ref, carry_ref):
    @pl.when(pl.program_id(1) == 0)
    def _():
        carry_ref[...] = jnp.zeros_like(carry_ref)

    tm = h_ref.shape[1]
    width = N_HEADS * LANES
    xn = _rmsnorm(h_ref[0], g_ref[...]).astype(BF16)
    qkv = jnp.dot(xn, wqkv_ref[...], preferred_element_type=F32)

    lane = lax.broadcasted_iota(jnp.int32, (tm, LANES), 1)

    def by_part(parts):
        out = jnp.zeros((tm, LANES), F32)
        for p in reversed(range(AUG_PARTS)):
            out = jnp.where(lane < (p + 1) * N_HEADS, parts[p], out)
        return out.astype(BF16)

    f_logit = jnp.dot(xn, wf_ref[...], preferred_element_type=F32) + bf_ref[...]
    log_f = jnp.minimum(f_logit, 0.0) - jnp.log1p(jnp.exp(-jnp.abs(f_logit)))
    parts = by_part([p.astype(F32) for p in _split_bf16x3(log_f)])
    t_out = lax.broadcasted_iota(jnp.int32, (tm, tm), 0)
    t_in = lax.broadcasted_iota(jnp.int32, (tm, tm), 1)
    tri = (t_in <= t_out).astype(BF16)
    sums = jnp.dot(tri, parts, preferred_element_type=F32)
    c = sums + carry_ref[...]
    for p in range(1, AUG_PARTS):
        c = c + pltpu.roll(sums, LANES - p * N_HEADS, axis=1)
    carry_ref[...] = c[tm - 1:tm, :]

    c_parts = [p.astype(F32) for p in _split_bf16x3(c * LOG2E)]
    c_parts = by_part([c_parts[0]] + [pltpu.roll(c_parts[p], p * N_HEADS, axis=1)
                                      for p in range(1, AUG_PARTS)])
    aug = jnp.dot(c_parts, place_ref[...], preferred_element_type=F32) + const_ref[...]

    for head in range(N_HEADS):
        half = _data_half(head)
        is_data = (lane >= half * HEAD_DIM) & (lane < (half + 1) * HEAD_DIM)
        src = (head // HEADS_PER_LANE_BLOCK) * LANES
        dst = slice(head * LANES, (head + 1) * LANES)
        q = qkv[:, src:src + LANES] * (QK_SCALE * LOG2E)
        k = qkv[:, D_MODEL + src:D_MODEL + src + LANES]
        v = qkv[:, 2 * D_MODEL + src:2 * D_MODEL + src + LANES]
        q_ref[0, :, dst] = jnp.where(is_data, q, aug[:, dst]).astype(BF16)
        k_ref[0, :, dst] = jnp.where(is_data, k, aug[:, width + head * LANES:
                                                      width + (head + 1) * LANES]).astype(BF16)
        v_ref[0, :, dst] = jnp.where(is_data, v, vconst_ref[half:half + 1, :]).astype(BF16)


def _fox_proj(h, g, w_in, b_forget):
    b, s, d = h.shape
    w_qkv = w_in[:, :3 * d].astype(BF16)
    pad = LANES - AUG_PARTS * N_HEADS
    w_f = jnp.pad(jnp.tile(w_in[:, 3 * d:], (1, AUG_PARTS)), ((0, 0), (0, pad))).astype(BF16)
    b_f = jnp.pad(jnp.tile(b_forget.astype(F32), AUG_PARTS), (0, pad)).reshape(1, LANES)
    place, const, v_const = _fox_aug_tables()
    tm = TOKEN_TILE
    out_spec = pl.BlockSpec((1, tm, N_HEADS * LANES), lambda i, j: (i, j, 0))
    out_shape = jax.ShapeDtypeStruct((b, s, N_HEADS * LANES), BF16)
    return pl.pallas_call(
        _fox_proj_kernel,
        grid=(b, s // tm),
        in_specs=[pl.BlockSpec((1, tm, d), lambda i, j: (i, j, 0)),
                  _resident((1, d)), _resident(w_qkv.shape), _resident(w_f.shape),
                  _resident(b_f.shape), _resident(place.shape), _resident(const.shape),
                  _resident(v_const.shape)],
        out_specs=[out_spec, out_spec, out_spec],
        out_shape=[out_shape, out_shape, out_shape],
        scratch_shapes=[pltpu.VMEM((1, LANES), F32)],
        compiler_params=_params("parallel", "arbitrary"),
        name="fox_proj",
    )(h, g.reshape(1, d), w_qkv, w_f, b_f, place, const, v_const)


def _fox_attn_kernel(q_ref, k_ref, v_ref, o_ref):
    seq = q_ref.shape[1]
    t = FOX_TILE
    half = t // 2
    heads = range(HEADS_PER_LANE_BLOCK)
    lane = lax.broadcasted_iota(jnp.int32, (half, LANES), 1)
    upper_mask = (lax.broadcasted_iota(jnp.int32, (half, half), 0)
                  >= lax.broadcasted_iota(jnp.int32, (half, half), 1))
    lower_mask = (lax.broadcasted_iota(jnp.int32, (half, t), 0) + half
                  >= lax.broadcasted_iota(jnp.int32, (half, t), 1))

    cols = [slice(hh * LANES, (hh + 1) * LANES) for hh in heads]

    jobs = []
    for qi in range(seq // t):
        q0 = qi * t
        for j in range(qi):
            jobs += [(qi, hh, (q0, t), (j * t, t), None) for hh in heads]
        for hh in heads:
            jobs.append((qi, hh, (q0, half), (q0, half), upper_mask))
            jobs.append((qi, hh, (q0 + half, half), (q0, t), lower_mask))

    state = {}
    finished = {}

    def scores(job):
        _, hh, (r0, rn), (k0, kn), mask = job
        s = lax.dot_general(q_ref[0, r0:r0 + rn, cols[hh]], k_ref[0, k0:k0 + kn, cols[hh]],
                            (((1,), (1,)), ((), ())), preferred_element_type=F32)
        return s if mask is None else jnp.where(mask, s, NEG_INF)

    def absorb(job, s):
        qi, hh, (r0, rn), (k0, kn), mask = job
        m, acc = state.get((qi, hh), (jnp.full((t, 1), NEG_INF, F32),
                                      jnp.zeros((t, LANES), F32)))
        if mask is not None:
            lo = r0 - qi * t
            m, acc = m[lo:lo + rn], acc[lo:lo + rn]
        m_new = jnp.maximum(m, jnp.max(s, axis=-1, keepdims=True))
        p = jnp.exp2(s - m_new).astype(BF16)
        acc = jnp.exp2(m - m_new) * acc + jnp.dot(p, v_ref[0, k0:k0 + kn, cols[hh]],
                                                  preferred_element_type=F32)
        if mask is None:
            state[(qi, hh)] = (m_new, acc)
            return
        accs = finished.setdefault(r0, {})
        accs[hh] = acc
        if len(accs) == len(heads):
            sums = [accs[h][:, _aug_base(h):_aug_base(h) + 1] for h in heads]
            o_ref[0, r0:r0 + rn, :] = (jnp.where(lane < HEAD_DIM, accs[0], accs[1])
                                       / jnp.where(lane < HEAD_DIM, sums[0], sums[1])).astype(BF16)

    pending = []
    for job in jobs:
        pending.append((job, scores(job)))
        if len(pending) > FOX_SCORES_AHEAD:
            absorb(*pending.pop(0))
    for item in pending:
        absorb(*item)


def _fox_attn(q, k, v):
    b, s, _ = q.shape
    pair_w = HEADS_PER_LANE_BLOCK * LANES
    blk = pl.BlockSpec((1, s, pair_w), lambda i, p: (i, 0, p))
    return pl.pallas_call(
        _fox_attn_kernel,
        grid=(b, N_HEADS // HEADS_PER_LANE_BLOCK),
        in_specs=[blk, blk, blk],
        out_specs=pl.BlockSpec((1, s, LANES), lambda i, p: (i, 0, p)),
        out_shape=jax.ShapeDtypeStruct((b, s, D_MODEL), BF16),
        compiler_params=_params("parallel", "parallel"),
        name="fox_attn",
    )(q, k, v)


def _lane_half_mask(lane, half):
    return (lane >= half * HEAD_DIM) & (lane < (half + 1) * HEAD_DIM)


def _swa_proj_kernel(h_ref, g_ref, w_ref, q_ref, k_ref, v_ref):
    tm = h_ref.shape[0]
    xn = _rmsnorm(h_ref[...], g_ref[...]).astype(BF16)
    qkv = jnp.dot(xn, w_ref[...], preferred_element_type=F32)
    kv_w = SWA_KV_HEADS * HEAD_DIM
    k_ref[...] = qkv[:, D_MODEL:D_MODEL + kv_w].astype(BF16)
    v_ref[...] = qkv[:, D_MODEL + kv_w:].astype(BF16)
    lane = lax.broadcasted_iota(jnp.int32, (tm, LANES), 1)
    for head in range(N_HEADS):
        kv = head // SWA_GROUP
        src = (head // HEADS_PER_LANE_BLOCK) * LANES
        q = qkv[:, src:src + LANES] * (QK_SCALE * LOG2E)
        if head % HEADS_PER_LANE_BLOCK != kv:
            q = pltpu.roll(q, HEAD_DIM, axis=1)
        q_ref[:, head * LANES:(head + 1) * LANES] = jnp.where(
            _lane_half_mask(lane, kv), q, 0.0).astype(BF16)


def _swa_proj(h, g, w_in):
    t, d = h.shape
    kv_w = SWA_KV_HEADS * HEAD_DIM
    tm = TOKEN_TILE
    return pl.pallas_call(
        _swa_proj_kernel,
        grid=(t // tm,),
        in_specs=[pl.BlockSpec((tm, d), lambda i: (i, 0)), _resident((1, d)),
                  _resident(w_in.shape)],
        out_specs=[pl.BlockSpec((tm, N_HEADS * LANES), lambda i: (i, 0)),
                   pl.BlockSpec((tm, kv_w), lambda i: (i, 0)),
                   pl.BlockSpec((tm, kv_w), lambda i: (i, 0))],
        out_shape=[jax.ShapeDtypeStruct((t, N_HEADS * LANES), BF16),
                   jax.ShapeDtypeStruct((t, kv_w), BF16),
                   jax.ShapeDtypeStruct((t, kv_w), BF16)],
        compiler_params=_params("parallel"),
        name="swa_proj",
    )(h, g.reshape(1, d), w_in.astype(BF16))


def _alibi_slope(head):
    return float(np.float32(2.0 ** (-8.0 * (head + 1) / N_HEADS)))


def _swa_attn_kernel(sinks_ref, q_ref, kp_ref, kc_ref, vp_ref, vc_ref, o_ref, bias_ref):
    w = WINDOW
    step = pl.program_id(1)

    @pl.when((pl.program_id(0) == 0) & (step == 0))
    def _():
        qi = lax.broadcasted_iota(jnp.int32, (w, 2 * w), 0)
        kj = lax.broadcasted_iota(jnp.int32, (w, 2 * w), 1)
        dist = qi + w - kj
        valid = (dist >= 0) & (dist < w)
        dist_f = dist.astype(F32)
        for head in range(N_HEADS):
            bias = jnp.where(valid, -_alibi_slope(head) * dist_f * LOG2E, NEG_INF)
            bias_ref[1, head] = bias
            bias_ref[0, head] = jnp.where(kj >= w, bias, NEG_INF)

    lane = lax.broadcasted_iota(jnp.int32, (w, LANES), 1)
    k_all = jnp.concatenate([kp_ref[0], kc_ref[0]], axis=0)
    v_all = jnp.concatenate([vp_ref[0], vc_ref[0]], axis=0)
    v_lane = lax.broadcasted_iota(jnp.int32, v_all.shape, 1)
    v_swapped = pltpu.roll(v_all, HEAD_DIM, axis=1)
    ones = jnp.ones_like(v_all)
    v_aug = []
    for kv in range(SWA_KV_HEADS):
        v_lo, v_hi = (v_all, v_swapped) if kv == 0 else (v_swapped, v_all)
        v_aug.append(jnp.concatenate([jnp.where(v_lane < HEAD_DIM, v_lo, ones),
                                      jnp.where(v_lane < HEAD_DIM, ones, v_hi)], axis=1))

    n_h = SWA_HEADS_PER_JOB
    jobs = [(i, h0) for i in range(SWA_BLOCKS_PER_STEP) for h0 in range(0, N_HEADS, n_h)]

    def scores(job):
        i, h0 = job
        qx = jnp.concatenate(
            [q_ref[0, i * w:(i + 1) * w, head * LANES:(head + 1) * LANES]
             for head in range(h0, h0 + n_h)], axis=0)
        return lax.dot_general(qx, k_all[i * w:(i + 2) * w], (((1,), (1,)), ((), ())),
                               preferred_element_type=F32)

    def absorb(job, s_all):
        i, h0 = job
        kv = h0 // SWA_GROUP
        table = jnp.minimum(step, 1) if i == 0 else 1
        p_rows, sink_p = [], []
        for g in range(n_h):
            sink = sinks_ref[h0 + g] * LOG2E
            s = s_all[g * w:(g + 1) * w] + bias_ref[table, h0 + g]
            m = jnp.maximum(jnp.max(s, axis=-1, keepdims=True), sink)
            p_rows.append(jnp.exp2(s - m).astype(BF16))
            sink_p.append(jnp.exp2(sink - m))
        acc = jnp.dot(jnp.concatenate(p_rows, axis=0), v_aug[kv][i * w:(i + 2) * w],
                      preferred_element_type=F32)
        for pair in range(n_h // HEADS_PER_LANE_BLOCK):
            num, den = [], []
            for d in range(HEADS_PER_LANE_BLOCK):
                g = pair * HEADS_PER_LANE_BLOCK + d
                a = acc[g * w:(g + 1) * w]
                num.append(a[:, d * LANES:(d + 1) * LANES])
                den.append(a[:, (1 - d) * LANES:(2 - d) * LANES] + sink_p[g])
            blk = h0 // HEADS_PER_LANE_BLOCK + pair
            o_ref[0, i * w:(i + 1) * w, blk * LANES:(blk + 1) * LANES] = (
                jnp.where(lane < HEAD_DIM, num[0], num[1])
                / jnp.where(lane < HEAD_DIM, den[0], den[1])).astype(BF16)

    pending = []
    for job in jobs:
        pending.append((job, scores(job)))
        if len(pending) > SWA_SCORES_AHEAD:
            absorb(*pending.pop(0))
    for item in pending:
        absorb(*item)


def _swa_attn(q, k, v, sinks):
    b, s, _ = q.shape
    w = WINDOW
    n = SWA_BLOCKS_PER_STEP
    kv_w = SWA_KV_HEADS * HEAD_DIM
    prev = pl.BlockSpec((1, w, kv_w), lambda i, j, *_: (i, jnp.maximum(j * n - 1, 0), 0))
    cur = pl.BlockSpec((1, n * w, kv_w), lambda i, j, *_: (i, j, 0))
    grid_spec = pltpu.PrefetchScalarGridSpec(
        num_scalar_prefetch=1,
        grid=(b, s // (n * w)),
        in_specs=[pl.BlockSpec((1, n * w, N_HEADS * LANES), lambda i, j, *_: (i, j, 0)),
                  prev, cur, prev, cur],
        out_specs=pl.BlockSpec((1, n * w, D_MODEL), lambda i, j, *_: (i, j, 0)),
        scratch_shapes=[pltpu.VMEM((2, N_HEADS, w, 2 * w), F32)],
    )
    return pl.pallas_call(
        _swa_attn_kernel,
        grid_spec=grid_spec,
        out_shape=jax.ShapeDtypeStruct((b, s, D_MODEL), BF16),
        compiler_params=_params("arbitrary", "arbitrary"),
        name="swa_attn",
    )(sinks.astype(F32), q, k, k, v, v)


def kernel(x, l0_ffn1_norm, l0_ffn1_w_gate, l0_ffn1_w_up, l0_ffn1_w_down, l0_mix_norm, l0_fox_w_in, l0_fox_b_forget, l0_fox_w_out, l0_ffn2_norm, l0_ffn2_w_gate, l0_ffn2_w_up, l0_ffn2_w_down, l1_ffn1_norm, l1_ffn1_w_gate, l1_ffn1_w_up, l1_ffn1_w_down, l1_mix_norm, l1_swa_w_in, l1_swa_sinks, l1_swa_w_out, l1_ffn2_norm, l1_ffn2_w_gate, l1_ffn2_w_up, l1_ffn2_w_down, final_norm):
    b, s, d = x.shape
    t = b * s
    h = x.reshape(t, d)

    h = _ffn(h, l0_ffn1_norm, l0_ffn1_w_gate, l0_ffn1_w_up, l0_ffn1_w_down)
    q, k, v = _fox_proj(h.reshape(b, s, d), l0_mix_norm, l0_fox_w_in, l0_fox_b_forget)
    o = _fox_attn(q, k, v)
    h = _ffn(h, l0_ffn2_norm, l0_ffn2_w_gate, l0_ffn2_w_up, l0_ffn2_w_down,
             attn=o.reshape(t, d), w_out=l0_fox_w_out)

    h = _ffn(h, l1_ffn1_norm, l1_ffn1_w_gate, l1_ffn1_w_up, l1_ffn1_w_down)
    q, k, v = _swa_proj(h, l1_mix_norm, l1_swa_w_in)
    kv_w = SWA_KV_HEADS * HEAD_DIM
    o = _swa_attn(q.reshape(b, s, N_HEADS * LANES), k.reshape(b, s, kv_w),
                  v.reshape(b, s, kv_w), l1_swa_sinks)
    h = _ffn(h, l1_ffn2_norm, l1_ffn2_w_gate, l1_ffn2_w_up, l1_ffn2_w_down,
             attn=o.reshape(t, d), w_out=l1_swa_w_out, final_g=final_norm)
    return h.reshape(b, s, d)
```

```python
import functools

import jax
import jax.numpy as jnp
import numpy as np
from jax import lax
from jax.experimental import pallas as pl
from jax.experimental.pallas import tpu as pltpu

F32 = jnp.float32
BF16 = jnp.bfloat16

D_MODEL = 1024
HEAD_DIM = 64
N_HEADS = D_MODEL // HEAD_DIM
SWA_KV_HEADS = 2
SWA_GROUP = N_HEADS // SWA_KV_HEADS
WINDOW = 128
RMS_EPS = 1e-6
NEG_INF = -1e30
QK_SCALE = HEAD_DIM ** -0.5

LANES = 128
HEADS_PER_LANE_BLOCK = LANES // HEAD_DIM
VMEM_LIMIT_BYTES = 56 * 1024 * 1024

TOKEN_TILE = 512
FFN_TILE = 1024
FFN_CHUNK = 512
FOX_TILE = 512
FOX_SCORES_AHEAD = 2
SWA_BLOCKS_PER_STEP = 4
SWA_HEADS_PER_JOB = 4
SWA_SCORES_AHEAD = 1


def _rmsnorm(x, g):
    ms = jnp.mean(x * x, axis=-1, keepdims=True)
    return x * lax.rsqrt(ms + RMS_EPS) * g


def _resident(shape):
    zeros = (0,) * len(shape)
    return pl.BlockSpec(shape, lambda *_: zeros, pipeline_mode=pl.Buffered(1))


def _params(*semantics):
    return pltpu.CompilerParams(dimension_semantics=semantics,
                                vmem_limit_bytes=VMEM_LIMIT_BYTES)


def _ffn_kernel(*refs, d_ff, mixer_out, final_norm):
    refs = list(refs)
    h_ref = refs.pop(0)
    attn_ref, wo_ref = (refs.pop(0), refs.pop(0)) if mixer_out else (None, None)
    g_ref, wg_ref, wu_ref, wd_ref = (refs.pop(0) for _ in range(4))
    gf_ref = refs.pop(0) if final_norm else None
    o_ref, a_ref = refs
    x = h_ref[...]
    if mixer_out:
        x = x + jnp.dot(attn_ref[...], wo_ref[...], preferred_element_type=F32)
    xn = _rmsnorm(x, g_ref[...]).astype(BF16)
    for c0 in range(0, d_ff, FFN_CHUNK):
        cw = min(FFN_CHUNK, d_ff - c0)
        gate = jnp.dot(xn, wg_ref[:, c0:c0 + cw], preferred_element_type=F32)
        up = jnp.dot(xn, wu_ref[:, c0:c0 + cw], preferred_element_type=F32)
        a_ref[:, c0:c0 + cw] = (gate * jax.nn.sigmoid(gate) * up).astype(BF16)
    y = jnp.dot(a_ref[...], wd_ref[...], preferred_element_type=F32)
    out = x + 0.5 * y
    if final_norm:
        out = _rmsnorm(out, gf_ref[...])
    o_ref[...] = out


def _ffn(h, g, wg, wu, wd, attn=None, w_out=None, final_g=None):
    t, d = h.shape
    d_ff = wg.shape[1]
    tile = pl.BlockSpec((FFN_TILE, d), lambda i: (i, 0))
    in_specs, args = [tile], [h]
    if attn is not None:
        in_specs += [pl.BlockSpec((FFN_TILE, attn.shape[1]), lambda i: (i, 0)),
                     _resident(w_out.shape)]
        args += [attn, w_out.astype(BF16)]
    in_specs += [_resident((1, d)), _resident((d, d_ff)), _resident((d, d_ff)),
                 _resident((d_ff, d))]
    args += [g.reshape(1, d), wg.astype(BF16), wu.astype(BF16), wd.astype(BF16)]
    if final_g is not None:
        in_specs.append(_resident((1, d)))
        args.append(final_g.reshape(1, d))
    return pl.pallas_call(
        functools.partial(_ffn_kernel, d_ff=d_ff, mixer_out=attn is not None,
                          final_norm=final_g is not None),
        grid=(t // FFN_TILE,),
        in_specs=in_specs,
        out_specs=tile,
        out_shape=jax.ShapeDtypeStruct((t, d), F32),
        scratch_shapes=[pltpu.VMEM((FFN_TILE, d_ff), BF16)],
        compiler_params=_params("parallel"),
        name="ffn",
    )(*args)


def _split_bf16x3(x):
    hi = x.astype(BF16)
    r = x - hi.astype(F32)
    mid = r.astype(BF16)
    lo = (r - mid.astype(F32)).astype(BF16)
    return hi, mid, lo


AUG_PARTS = 3
LOG2E = float(np.log2(np.e))


def _data_half(head):
    return head % HEADS_PER_LANE_BLOCK


def _aug_base(head):
    return HEAD_DIM * (1 - _data_half(head))


def _fox_aug_tables():
    width = N_HEADS * LANES
    place = np.zeros((LANES, 2 * width), np.float32)
    const = np.zeros((1, 2 * width), np.float32)
    v_const = np.zeros((HEADS_PER_LANE_BLOCK, LANES), np.float32)
    for head in range(N_HEADS):
        base = head * LANES + _aug_base(head)
        for part in range(AUG_PARTS):
            place[part * N_HEADS + head, base + part] = 1.0
            place[part * N_HEADS + head, width + base + AUG_PARTS + part] = -1.0
            const[0, base + AUG_PARTS + part] = 1.0
            const[0, width + base + part] = 1.0
    for half in range(HEADS_PER_LANE_BLOCK):
        v_const[half, HEAD_DIM * (1 - half)] = 1.0
    return jnp.asarray(place, BF16), jnp.asarray(const, F32), jnp.asarray(v_const, F32)


def _fox_proj_kernel(h_ref, g_ref, wqkv_ref, wf_ref, bf_ref, place_ref, const_ref, vconst_ref,
                     q_ref, k_ref, v_ref, carry_ref):
    @pl.when(pl.program_id(1) == 0)
    def _():
        carry_ref[...] = jnp.zeros_like(carry_ref)

    tm = h_ref.shape[1]
    width = N_HEADS * LANES
    xn = _rmsnorm(h_ref[0], g_ref[...]).astype(BF16)
    lane = lax.broadcasted_iota(jnp.int32, (tm, LANES), 1)

    def project(which):
        return jnp.dot(xn, wqkv_ref[:, which * D_MODEL:(which + 1) * D_MODEL],
                       preferred_element_type=F32)

    def head_blocks(proj, fill):
        for head in range(N_HEADS):
            src = (head // HEADS_PER_LANE_BLOCK) * LANES
            dst = slice(head * LANES, (head + 1) * LANES)
            yield dst, jnp.where(_lane_half_mask(lane, _data_half(head)),
                                 proj[:, src:src + LANES], fill(head, dst)).astype(BF16)

    def by_part(parts):
        out = jnp.zeros((tm, LANES), F32)
        for p in reversed(range(AUG_PARTS)):
            out = jnp.where(lane < (p + 1) * N_HEADS, parts[p], out)
        return out.astype(BF16)


    f_logit = jnp.dot(xn, wf_ref[...], preferred_element_type=F32) + bf_ref[...]
    v_proj = project(2)
    log_f = jnp.minimum(f_logit, 0.0) - jnp.log1p(jnp.exp(-jnp.abs(f_logit)))
    parts = by_part([p.astype(F32) for p in _split_bf16x3(log_f)])
    t_out = lax.broadcasted_iota(jnp.int32, (tm, tm), 0)
    t_in = lax.broadcasted_iota(jnp.int32, (tm, tm), 1)
    tri = (t_in <= t_out).astype(BF16)
    sums = jnp.dot(tri, parts, preferred_element_type=F32)
    q_proj = project(0) * (QK_SCALE * LOG2E)
    for dst, blk in head_blocks(v_proj, lambda head, dst: vconst_ref[_data_half(head):
                                                                   _data_half(head) + 1, :]):
        v_ref[0, :, dst] = blk

    c = sums + carry_ref[...]
    for p in range(1, AUG_PARTS):
        c = c + pltpu.roll(sums, LANES - p * N_HEADS, axis=1)
    carry_ref[...] = c[tm - 1:tm, :]
    c_parts = [p.astype(F32) for p in _split_bf16x3(c * LOG2E)]
    c_parts = by_part([c_parts[0]] + [pltpu.roll(c_parts[p], p * N_HEADS, axis=1)
                                      for p in range(1, AUG_PARTS)])
    aug = jnp.dot(c_parts, place_ref[...], preferred_element_type=F32) + const_ref[...]
    k_proj = project(1)
    for dst, blk in head_blocks(q_proj, lambda head, dst: aug[:, dst]):
        q_ref[0, :, dst] = blk
    for dst, blk in head_blocks(k_proj, lambda head, dst: aug[:, width + dst.start:
                                                               width + dst.stop]):
        k_ref[0, :, dst] = blk


def _fox_proj(h, g, w_in, b_forget):
    b, s, d = h.shape
    w_qkv = w_in[:, :3 * d].astype(BF16)
    pad = LANES - AUG_PARTS * N_HEADS
    w_f = jnp.pad(jnp.tile(w_in[:, 3 * d:], (1, AUG_PARTS)), ((0, 0), (0, pad))).astype(BF16)
    b_f = jnp.pad(jnp.tile(b_forget.astype(F32), AUG_PARTS), (0, pad)).reshape(1, LANES)
    place, const, v_const = _fox_aug_tables()
    tm = TOKEN_TILE
    out_spec = pl.BlockSpec((1, tm, N_HEADS * LANES), lambda i, j: (i, j, 0))
    out_shape = jax.ShapeDtypeStruct((b, s, N_HEADS * LANES), BF16)
    return pl.pallas_call(
        _fox_proj_kernel,
        grid=(b, s // tm),
        in_specs=[pl.BlockSpec((1, tm, d), lambda i, j: (i, j, 0)),
                  _resident((1, d)), _resident(w_qkv.shape), _resident(w_f.shape),
                  _resident(b_f.shape), _resident(place.shape), _resident(const.shape),
                  _resident(v_const.shape)],
        out_specs=[out_spec, out_spec, out_spec],
        out_shape=[out_shape, out_shape, out_shape],
        scratch_shapes=[pltpu.VMEM((1, LANES), F32)],
        compiler_params=_params("parallel", "arbitrary"),
        name="fox_proj",
    )(h, g.reshape(1, d), w_qkv, w_f, b_f, place, const, v_const)


def _fox_attn_kernel(q_ref, k_ref, v_ref, o_ref):
    seq = q_ref.shape[1]
    t = FOX_TILE
    half = t // 2
    heads = range(HEADS_PER_LANE_BLOCK)
    lane = lax.broadcasted_iota(jnp.int32, (half, LANES), 1)
    upper_mask = (lax.broadcasted_iota(jnp.int32, (half, half), 0)
                  >= lax.broadcasted_iota(jnp.int32, (half, half), 1))
    lower_mask = (lax.broadcasted_iota(jnp.int32, (half, t), 0) + half
                  >= lax.broadcasted_iota(jnp.int32, (half, t), 1))

    cols = [slice(hh * LANES, (hh + 1) * LANES) for hh in heads]

    jobs = []
    for qi in range(seq // t):
        q0 = qi * t
        for j in range(qi):
            jobs += [(qi, hh, (q0, t), (j * t, t), None) for hh in heads]
        for hh in heads:
            jobs.append((qi, hh, (q0, half), (q0, half), upper_mask))
            jobs.append((qi, hh, (q0 + half, half), (q0, t), lower_mask))

    state = {}
    finished = {}

    def scores(job):
        _, hh, (r0, rn), (k0, kn), mask = job
        s = lax.dot_general(q_ref[0, r0:r0 + rn, cols[hh]], k_ref[0, k0:k0 + kn, cols[hh]],
                            (((1,), (1,)), ((), ())), preferred_element_type=F32)
        return s if mask is None else jnp.where(mask, s, NEG_INF)

    def absorb(job, s):
        qi, hh, (r0, rn), (k0, kn), mask = job
        m, acc = state.get((qi, hh), (jnp.full((t, 1), NEG_INF, F32),
                                      jnp.zeros((t, LANES), F32)))
        if mask is not None:
            lo = r0 - qi * t
            m, acc = m[lo:lo + rn], acc[lo:lo + rn]
        m_new = jnp.maximum(m, jnp.max(s, axis=-1, keepdims=True))
        p = jnp.exp2(s - m_new).astype(BF16)
        acc = jnp.exp2(m - m_new) * acc + jnp.dot(p, v_ref[0, k0:k0 + kn, cols[hh]],
                                                  preferred_element_type=F32)
        if mask is None:
            state[(qi, hh)] = (m_new, acc)
            return
        accs = finished.setdefault(r0, {})
        accs[hh] = acc
        if len(accs) == len(heads):
            sums = [accs[h][:, _aug_base(h):_aug_base(h) + 1] for h in heads]
            o_ref[0, r0:r0 + rn, :] = (jnp.where(lane < HEAD_DIM, accs[0], accs[1])
                                       / jnp.where(lane < HEAD_DIM, sums[0], sums[1])).astype(BF16)

    pending = []
    for job in jobs:
        pending.append((job, scores(job)))
        if len(pending) > FOX_SCORES_AHEAD:
            absorb(*pending.pop(0))
    for item in pending:
        absorb(*item)


def _fox_attn(q, k, v):
    b, s, _ = q.shape
    pair_w = HEADS_PER_LANE_BLOCK * LANES
    blk = pl.BlockSpec((1, s, pair_w), lambda i, p: (i, 0, p))
    return pl.pallas_call(
        _fox_attn_kernel,
        grid=(b, N_HEADS // HEADS_PER_LANE_BLOCK),
        in_specs=[blk, blk, blk],
        out_specs=pl.BlockSpec((1, s, LANES), lambda i, p: (i, 0, p)),
        out_shape=jax.ShapeDtypeStruct((b, s, D_MODEL), BF16),
        compiler_params=_params("parallel", "parallel"),
        name="fox_attn",
    )(q, k, v)


def _lane_half_mask(lane, half):
    return (lane >= half * HEAD_DIM) & (lane < (half + 1) * HEAD_DIM)


def _swa_proj_kernel(h_ref, g_ref, w_ref, q_ref, k_ref, v_ref):
    tm = h_ref.shape[0]
    xn = _rmsnorm(h_ref[...], g_ref[...]).astype(BF16)
    qkv = jnp.dot(xn, w_ref[...], preferred_element_type=F32)
    kv_w = SWA_KV_HEADS * HEAD_DIM
    k_ref[...] = qkv[:, D_MODEL:D_MODEL + kv_w].astype(BF16)
    v_ref[...] = qkv[:, D_MODEL + kv_w:].astype(BF16)
    lane = lax.broadcasted_iota(jnp.int32, (tm, LANES), 1)
    for head in range(N_HEADS):
        kv = head // SWA_GROUP
        src = (head // HEADS_PER_LANE_BLOCK) * LANES
        q = qkv[:, src:src + LANES] * (QK_SCALE * LOG2E)
        if head % HEADS_PER_LANE_BLOCK != kv:
            q = pltpu.roll(q, HEAD_DIM, axis=1)
        q_ref[:, head * LANES:(head + 1) * LANES] = jnp.where(
            _lane_half_mask(lane, kv), q, 0.0).astype(BF16)


def _swa_proj(h, g, w_in):
    t, d = h.shape
    kv_w = SWA_KV_HEADS * HEAD_DIM
    tm = TOKEN_TILE
    return pl.pallas_call(
        _swa_proj_kernel,
        grid=(t // tm,),
        in_specs=[pl.BlockSpec((tm, d), lambda i: (i, 0)), _resident((1, d)),
                  _resident(w_in.shape)],
        out_specs=[pl.BlockSpec((tm, N_HEADS * LANES), lambda i: (i, 0)),
                   pl.BlockSpec((tm, kv_w), lambda i: (i, 0)),
                   pl.BlockSpec((tm, kv_w), lambda i: (i, 0))],
        out_shape=[jax.ShapeDtypeStruct((t, N_HEADS * LANES), BF16),
                   jax.ShapeDtypeStruct((t, kv_w), BF16),
                   jax.ShapeDtypeStruct((t, kv_w), BF16)],
        compiler_params=_params("parallel"),
        name="swa_proj",
    )(h, g.reshape(1, d), w_in.astype(BF16))


def _alibi_slope(head):
    return float(np.float32(2.0 ** (-8.0 * (head + 1) / N_HEADS)))


def _swa_attn_kernel(sinks_ref, q_ref, kp_ref, kc_ref, vp_ref, vc_ref, o_ref, bias_ref):
    w = WINDOW
    step = pl.program_id(1)

    @pl.when((pl.program_id(0) == 0) & (step == 0))
    def _():
        qi = lax.broadcasted_iota(jnp.int32, (w, 2 * w), 0)
        kj = lax.broadcasted_iota(jnp.int32, (w, 2 * w), 1)
        dist = qi + w - kj
        valid = (dist >= 0) & (dist < w)
        dist_f = dist.astype(F32)
        for head in range(N_HEADS):
            bias = jnp.where(valid, -_alibi_slope(head) * dist_f * LOG2E, NEG_INF)
            bias_ref[1, head] = bias
            bias_ref[0, head] = jnp.where(kj >= w, bias, NEG_INF)

    lane = lax.broadcasted_iota(jnp.int32, (w, LANES), 1)
    k_all = jnp.concatenate([kp_ref[0], kc_ref[0]], axis=0)
    v_all = jnp.concatenate([vp_ref[0], vc_ref[0]], axis=0)
    v_lane = lax.broadcasted_iota(jnp.int32, v_all.shape, 1)
    v_swapped = pltpu.roll(v_all, HEAD_DIM, axis=1)
    ones = jnp.ones_like(v_all)
    v_aug = []
    for kv in range(SWA_KV_HEADS):
        v_lo, v_hi = (v_all, v_swapped) if kv == 0 else (v_swapped, v_all)
        v_aug.append(jnp.concatenate([jnp.where(v_lane < HEAD_DIM, v_lo, ones),
                                      jnp.where(v_lane < HEAD_DIM, ones, v_hi)], axis=1))

    n_h = SWA_HEADS_PER_JOB
    jobs = [(i, h0) for i in range(SWA_BLOCKS_PER_STEP) for h0 in range(0, N_HEADS, n_h)]

    def scores(job):
        i, h0 = job
        qx = jnp.concatenate(
            [q_ref[0, i * w:(i + 1) * w, head * LANES:(head + 1) * LANES]
             for head in range(h0, h0 + n_h)], axis=0)
        return lax.dot_general(qx, k_all[i * w:(i + 2) * w], (((1,), (1,)), ((), ())),
                               preferred_element_type=F32)

    def absorb(job, s_all):
        i, h0 = job
        kv = h0 // SWA_GROUP
        table = jnp.minimum(step, 1) if i == 0 else 1
        p_rows, sink_p = [], []
        for g in range(n_h):
            sink = sinks_ref[h0 + g] * LOG2E
            s = s_all[g * w:(g + 1) * w] + bias_ref[table, h0 + g]
            m = jnp.maximum(jnp.max(s, axis=-1, keepdims=True), sink)
            p_rows.append(jnp.exp2(s - m).astype(BF16))
            sink_p.append(jnp.exp2(sink - m))
        acc = jnp.dot(jnp.concatenate(p_rows, axis=0), v_aug[kv][i * w:(i + 2) * w],
                      preferred_element_type=F32)
        for pair in range(n_h // HEADS_PER_LANE_BLOCK):
            num, den = [], []
            for d in range(HEADS_PER_LANE_BLOCK):
                g = pair * HEADS_PER_LANE_BLOCK + d
                a = acc[g * w:(g + 1) * w]
                num.append(a[:, d * LANES:(d + 1) * LANES])
                den.append(a[:, (1 - d) * LANES:(2 - d) * LANES] + sink_p[g])
            blk = h0 // HEADS_PER_LANE_BLOCK + pair
            o_ref[0, i * w:(i + 1) * w, blk * LANES:(blk + 1) * LANES] = (
                jnp.where(lane < HEAD_DIM, num[0], num[1])
                / jnp.where(lane < HEAD_DIM, den[0], den[1])).astype(BF16)

    pending = []
    for job in jobs:
        pending.append((job, scores(job)))
        if len(pending) > SWA_SCORES_AHEAD:
            absorb(*pending.pop(0))
    for item in pending:
        absorb(*item)


def _swa_attn(q, k, v, sinks):
    b, s, _ = q.shape
    w = WINDOW
    n = SWA_BLOCKS_PER_STEP
    kv_w = SWA_KV_HEADS * HEAD_DIM
    prev = pl.BlockSpec((1, w, kv_w), lambda i, j, *_: (i, jnp.maximum(j * n - 1, 0), 0))
    cur = pl.BlockSpec((1, n * w, kv_w), lambda i, j, *_: (i, j, 0))
    grid_spec = pltpu.PrefetchScalarGridSpec(
        num_scalar_prefetch=1,
        grid=(b, s // (n * w)),
        in_specs=[pl.BlockSpec((1, n * w, N_HEADS * LANES), lambda i, j, *_: (i, j, 0)),
                  prev, cur, prev, cur],
        out_specs=pl.BlockSpec((1, n * w, D_MODEL), lambda i, j, *_: (i, j, 0)),
        scratch_shapes=[pltpu.VMEM((2, N_HEADS, w, 2 * w), F32)],
    )
    return pl.pallas_call(
        _swa_attn_kernel,
        grid_spec=grid_spec,
        out_shape=jax.ShapeDtypeStruct((b, s, D_MODEL), BF16),
        compiler_params=_params("arbitrary", "arbitrary"),
        name="swa_attn",
    )(sinks.astype(F32), q, k, k, v, v)


def kernel(x, l0_ffn1_norm, l0_ffn1_w_gate, l0_ffn1_w_up, l0_ffn1_w_down, l0_mix_norm, l0_fox_w_in, l0_fox_b_forget, l0_fox_w_out, l0_ffn2_norm, l0_ffn2_w_gate, l0_ffn2_w_up, l0_ffn2_w_down, l1_ffn1_norm, l1_ffn1_w_gate, l1_ffn1_w_up, l1_ffn1_w_down, l1_mix_norm, l1_swa_w_in, l1_swa_sinks, l1_swa_w_out, l1_ffn2_norm, l1_ffn2_w_gate, l1_ffn2_w_up, l1_ffn2_w_down, final_norm):
    b, s, d = x.shape
    t = b * s
    h = x.reshape(t, d)

    h = _ffn(h, l0_ffn1_norm, l0_ffn1_w_gate, l0_ffn1_w_up, l0_ffn1_w_down)
    q, k, v = _fox_proj(h.reshape(b, s, d), l0_mix_norm, l0_fox_w_in, l0_fox_b_forget)
    o = _fox_attn(q, k, v)
    h = _ffn(h, l0_ffn2_norm, l0_ffn2_w_gate, l0_ffn2_w_up, l0_ffn2_w_down,
             attn=o.reshape(t, d), w_out=l0_fox_w_out)

    h = _ffn(h, l1_ffn1_norm, l1_ffn1_w_gate, l1_ffn1_w_up, l1_ffn1_w_down)
    q, k, v = _swa_proj(h, l1_mix_norm, l1_swa_w_in)
    kv_w = SWA_KV_HEADS * HEAD_DIM
    o = _swa_attn(q.reshape(b, s, N_HEADS * LANES), k.reshape(b, s, kv_w),
                  v.reshape(b, s, kv_w), l1_swa_sinks)
    h = _ffn(h, l1_ffn2_norm, l1_ffn2_w_gate, l1_ffn2_w_up, l1_ffn2_w_down,
             attn=o.reshape(t, d), w_out=l1_swa_w_out, final_g=final_norm)
    return h.reshape(b, s, d)
```

```python
import functools

import jax
import jax.numpy as jnp
import numpy as np
from jax import lax
from jax.experimental import pallas as pl
from jax.experimental.pallas import tpu as pltpu

F32 = jnp.float32
BF16 = jnp.bfloat16

D_MODEL = 1024
HEAD_DIM = 64
N_HEADS = D_MODEL // HEAD_DIM
SWA_KV_HEADS = 2
SWA_GROUP = N_HEADS // SWA_KV_HEADS
WINDOW = 128
RMS_EPS = 1e-6
NEG_INF = -1e30
QK_SCALE = HEAD_DIM ** -0.5

LANES = 128
BF16_TILE_ROWS = 16
CAST_BLOCK_ROWS = 128
HEADS_PER_LANE_BLOCK = LANES // HEAD_DIM
VMEM_LIMIT_BYTES = 56 * 1024 * 1024

TOKEN_TILE = 512
FFN_TILE = 1024
FFN_CHUNK = 512
FOX_TILE = 512
FOX_SCORES_AHEAD = 2
SWA_BLOCKS_PER_STEP = 4
SWA_HEADS_PER_JOB = 4
SWA_SCORES_AHEAD = 1


def _rmsnorm(x, g):
    ms = jnp.mean(x * x, axis=-1, keepdims=True)
    return x * lax.rsqrt(ms + RMS_EPS) * g


def _resident(shape):
    zeros = (0,) * len(shape)
    return pl.BlockSpec(shape, lambda *_: zeros, pipeline_mode=pl.Buffered(1))


def _params(*semantics):
    return pltpu.CompilerParams(dimension_semantics=semantics,
                                vmem_limit_bytes=VMEM_LIMIT_BYTES)


def _ffn_kernel(*refs, d_ff, mixer_out, final_norm, n_casts):
    refs = list(refs)
    h_ref = refs.pop(0)
    attn_ref, wo_ref = (refs.pop(0), refs.pop(0)) if mixer_out else (None, None)
    g_ref, wg_ref, wu_ref, wd_ref = (refs.pop(0) for _ in range(4))
    gf_ref = refs.pop(0) if final_norm else None
    cast_in = [refs.pop(0) for _ in range(n_casts)]
    o_ref = refs.pop(0)
    cast_out = [refs.pop(0) for _ in range(n_casts)]
    a_ref, = refs
    for src, dst in zip(cast_in, cast_out):
        dst[...] = src[...].astype(BF16)
    x = h_ref[...]
    if mixer_out:
        x = x + jnp.dot(attn_ref[...], wo_ref[...], preferred_element_type=F32)
    xn = _rmsnorm(x, g_ref[...]).astype(BF16)
    for c0 in range(0, d_ff, FFN_CHUNK):
        cw = min(FFN_CHUNK, d_ff - c0)
        gate = jnp.dot(xn, wg_ref[:, c0:c0 + cw], preferred_element_type=F32)
        up = jnp.dot(xn, wu_ref[:, c0:c0 + cw], preferred_element_type=F32)
        a_ref[:, c0:c0 + cw] = (gate * jax.nn.sigmoid(gate) * up).astype(BF16)
    y = jnp.dot(a_ref[...], wd_ref[...], preferred_element_type=F32)
    out = x + 0.5 * y
    if final_norm:
        out = _rmsnorm(out, gf_ref[...])
    o_ref[...] = out


def _cast_spec(rows, cols, n_steps):
    block = rows // n_steps
    if rows % n_steps or block % BF16_TILE_ROWS:
        block = CAST_BLOCK_ROWS
    last = rows // block - 1
    return pl.BlockSpec((block, cols), lambda i: (jnp.minimum(i, last), 0))


def _ffn(h, g, wg, wu, wd, attn=None, w_out=None, final_g=None, casts=()):
    t, d = h.shape
    d_ff = wg.shape[1]
    n_steps = t // FFN_TILE
    tile = pl.BlockSpec((FFN_TILE, d), lambda i: (i, 0))
    in_specs, args = [tile], [h]
    if attn is not None:
        in_specs += [pl.BlockSpec((FFN_TILE, attn.shape[1]), lambda i: (i, 0)),
                     _resident(w_out.shape)]
        args += [attn, w_out.astype(BF16)]
    in_specs += [_resident((1, d)), _resident((d, d_ff)), _resident((d, d_ff)),
                 _resident((d_ff, d))]
    args += [g.reshape(1, d), wg.astype(BF16), wu.astype(BF16), wd.astype(BF16)]
    if final_g is not None:
        in_specs.append(_resident((1, d)))
        args.append(final_g.reshape(1, d))
    cast_specs = [_cast_spec(w.shape[0], cols, n_steps) for w, cols in casts]
    outs = pl.pallas_call(
        functools.partial(_ffn_kernel, d_ff=d_ff, mixer_out=attn is not None,
                          final_norm=final_g is not None, n_casts=len(casts)),
        grid=(n_steps,),
        in_specs=in_specs + cast_specs,
        out_specs=[tile] + cast_specs,
        out_shape=[jax.ShapeDtypeStruct((t, d), F32)]
        + [jax.ShapeDtypeStruct((w.shape[0], cols), BF16) for w, cols in casts],
        scratch_shapes=[pltpu.VMEM((FFN_TILE, d_ff), BF16)],
        compiler_params=_params("arbitrary"),
        name="ffn",
    )(*args, *(w for w, _ in casts))
    return outs[0], outs[1:]


def _split_bf16x3(x):
    hi = x.astype(BF16)
    r = x - hi.astype(F32)
    mid = r.astype(BF16)
    lo = (r - mid.astype(F32)).astype(BF16)
    return hi, mid, lo


AUG_PARTS = 3
LOG2E = float(np.log2(np.e))


def _data_half(head):
    return head % HEADS_PER_LANE_BLOCK


def _aug_base(head):
    return HEAD_DIM * (1 - _data_half(head))


def _fox_aug_tables():
    width = N_HEADS * LANES
    place = np.zeros((LANES, 2 * width), np.float32)
    const = np.zeros((1, 2 * width), np.float32)
    v_const = np.zeros((HEADS_PER_LANE_BLOCK, LANES), np.float32)
    for head in range(N_HEADS):
        base = head * LANES + _aug_base(head)
        for part in range(AUG_PARTS):
            place[part * N_HEADS + head, base + part] = 1.0
            place[part * N_HEADS + head, width + base + AUG_PARTS + part] = -1.0
            const[0, base + AUG_PARTS + part] = 1.0
            const[0, width + base + part] = 1.0
    for half in range(HEADS_PER_LANE_BLOCK):
        v_const[half, HEAD_DIM * (1 - half)] = 1.0
    return jnp.asarray(place, BF16), jnp.asarray(const, F32), jnp.asarray(v_const, F32)


def _fox_proj_kernel(h_ref, g_ref, wqkv_ref, wf_ref, bf_ref, place_ref, const_ref, vconst_ref,
                     q_ref, k_ref, v_ref, carry_ref):
    @pl.when(pl.program_id(1) == 0)
    def _():
        carry_ref[...] = jnp.zeros_like(carry_ref)

    tm = h_ref.shape[1]
    width = N_HEADS * LANES
    xn = _rmsnorm(h_ref[0], g_ref[...]).astype(BF16)
    lane = lax.broadcasted_iota(jnp.int32, (tm, LANES), 1)

    def project(which):
        return jnp.dot(xn, wqkv_ref[:, which * D_MODEL:(which + 1) * D_MODEL],
                       preferred_element_type=F32)

    def head_blocks(proj, fill):
        for head in range(N_HEADS):
            src = (head // HEADS_PER_LANE_BLOCK) * LANES
            dst = slice(head * LANES, (head + 1) * LANES)
            yield dst, jnp.where(_lane_half_mask(lane, _data_half(head)),
                                 proj[:, src:src + LANES], fill(head, dst)).astype(BF16)

    def by_part(parts):
        out = jnp.zeros((tm, LANES), F32)
        for p in reversed(range(AUG_PARTS)):
            out = jnp.where(lane < (p + 1) * N_HEADS, parts[p], out)
        return out.astype(BF16)


    f_logit = jnp.dot(xn, wf_ref[...], preferred_element_type=F32) + bf_ref[...]
    v_proj = project(2)
    log_f = jnp.minimum(f_logit, 0.0) - jnp.log1p(jnp.exp(-jnp.abs(f_logit)))
    parts = by_part([p.astype(F32) for p in _split_bf16x3(log_f)])
    t_out = lax.broadcasted_iota(jnp.int32, (tm, tm), 0)
    t_in = lax.broadcasted_iota(jnp.int32, (tm, tm), 1)
    tri = (t_in <= t_out).astype(BF16)
    sums = jnp.dot(tri, parts, preferred_element_type=F32)
    q_proj = project(0) * (QK_SCALE * LOG2E)
    for dst, blk in head_blocks(v_proj, lambda head, dst: vconst_ref[_data_half(head):
                                                                   _data_half(head) + 1, :]):
        v_ref[0, :, dst] = blk

    c = sums + carry_ref[...]
    for p in range(1, AUG_PARTS):
        c = c + pltpu.roll(sums, LANES - p * N_HEADS, axis=1)
    carry_ref[...] = c[tm - 1:tm, :]
    c_parts = [p.astype(F32) for p in _split_bf16x3(c * LOG2E)]
    c_parts = by_part([c_parts[0]] + [pltpu.roll(c_parts[p], p * N_HEADS, axis=1)
                                      for p in range(1, AUG_PARTS)])
    aug = jnp.dot(c_parts, place_ref[...], preferred_element_type=F32) + const_ref[...]
    k_proj = project(1)
    for dst, blk in head_blocks(q_proj, lambda head, dst: aug[:, dst]):
        q_ref[0, :, dst] = blk
    for dst, blk in head_blocks(k_proj, lambda head, dst: aug[:, width + dst.start:
                                                               width + dst.stop]):
        k_ref[0, :, dst] = blk


def _fox_proj(h, g, w_in, w_qkv, b_forget):
    b, s, d = h.shape
    pad = LANES - AUG_PARTS * N_HEADS
    w_f = jnp.pad(jnp.tile(w_in[:, 3 * d:], (1, AUG_PARTS)), ((0, 0), (0, pad))).astype(BF16)
    b_f = jnp.pad(jnp.tile(b_forget.astype(F32), AUG_PARTS), (0, pad)).reshape(1, LANES)
    place, const, v_const = _fox_aug_tables()
    tm = TOKEN_TILE
    out_spec = pl.BlockSpec((1, tm, N_HEADS * LANES), lambda i, j: (i, j, 0))
    out_shape = jax.ShapeDtypeStruct((b, s, N_HEADS * LANES), BF16)
    return pl.pallas_call(
        _fox_proj_kernel,
        grid=(b, s // tm),
        in_specs=[pl.BlockSpec((1, tm, d), lambda i, j: (i, j, 0)),
                  _resident((1, d)), _resident(w_qkv.shape), _resident(w_f.shape),
                  _resident(b_f.shape), _resident(place.shape), _resident(const.shape),
                  _resident(v_const.shape)],
        out_specs=[out_spec, out_spec, out_spec],
        out_shape=[out_shape, out_shape, out_shape],
        scratch_shapes=[pltpu.VMEM((1, LANES), F32)],
        compiler_params=_params("parallel", "arbitrary"),
        name="fox_proj",
    )(h, g.reshape(1, d), w_qkv, w_f, b_f, place, const, v_const)


def _fox_attn_kernel(q_ref, k_ref, v_ref, o_ref):
    seq = q_ref.shape[1]
    t = FOX_TILE
    half = t // 2
    heads = range(HEADS_PER_LANE_BLOCK)
    lane = lax.broadcasted_iota(jnp.int32, (half, LANES), 1)
    upper_mask = (lax.broadcasted_iota(jnp.int32, (half, half), 0)
                  >= lax.broadcasted_iota(jnp.int32, (half, half), 1))
    lower_mask = (lax.broadcasted_iota(jnp.int32, (half, t), 0) + half
                  >= lax.broadcasted_iota(jnp.int32, (half, t), 1))

    cols = [slice(hh * LANES, (hh + 1) * LANES) for hh in heads]

    jobs = []
    for qi in range(seq // t):
        q0 = qi * t
        for j in range(qi):
            jobs += [(qi, hh, (q0, t), (j * t, t), None) for hh in heads]
        for hh in heads:
            jobs.append((qi, hh, (q0, half), (q0, half), upper_mask))
            jobs.append((qi, hh, (q0 + half, half), (q0, t), lower_mask))

    state = {}
    finished = {}

    def scores(job):
        _, hh, (r0, rn), (k0, kn), mask = job
        s = lax.dot_general(q_ref[0, r0:r0 + rn, cols[hh]], k_ref[0, k0:k0 + kn, cols[hh]],
                            (((1,), (1,)), ((), ())), preferred_element_type=F32)
        return s if mask is None else jnp.where(mask, s, NEG_INF)

    def absorb(job, s):
        qi, hh, (r0, rn), (k0, kn), mask = job
        m, acc = state.get((qi, hh), (jnp.full((t, 1), NEG_INF, F32),
                                      jnp.zeros((t, LANES), F32)))
        if mask is not None:
            lo = r0 - qi * t
            m, acc = m[lo:lo + rn], acc[lo:lo + rn]
        m_new = jnp.maximum(m, jnp.max(s, axis=-1, keepdims=True))
        p = jnp.exp2(s - m_new).astype(BF16)
        acc = jnp.exp2(m - m_new) * acc + jnp.dot(p, v_ref[0, k0:k0 + kn, cols[hh]],
                                                  preferred_element_type=F32)
        if mask is None:
            state[(qi, hh)] = (m_new, acc)
            return
        accs = finished.setdefault(r0, {})
        accs[hh] = acc
        if len(accs) == len(heads):
            sums = [accs[h][:, _aug_base(h):_aug_base(h) + 1] for h in heads]
            o_ref[0, r0:r0 + rn, :] = (jnp.where(lane < HEAD_DIM, accs[0], accs[1])
                                       / jnp.where(lane < HEAD_DIM, sums[0], sums[1])).astype(BF16)

    pending = []
    for job in jobs:
        pending.append((job, scores(job)))
        if len(pending) > FOX_SCORES_AHEAD:
            absorb(*pending.pop(0))
    for item in pending:
        absorb(*item)


def _fox_attn(q, k, v):
    b, s, _ = q.shape
    pair_w = HEADS_PER_LANE_BLOCK * LANES
    blk = pl.BlockSpec((1, s, pair_w), lambda i, p: (i, 0, p))
    return pl.pallas_call(
        _fox_attn_kernel,
        grid=(b, N_HEADS // HEADS_PER_LANE_BLOCK),
        in_specs=[blk, blk, blk],
        out_specs=pl.BlockSpec((1, s, LANES), lambda i, p: (i, 0, p)),
        out_shape=jax.ShapeDtypeStruct((b, s, D_MODEL), BF16),
        compiler_params=_params("parallel", "parallel"),
        name="fox_attn",
    )(q, k, v)


def _lane_half_mask(lane, half):
    return (lane >= half * HEAD_DIM) & (lane < (half + 1) * HEAD_DIM)


def _swa_proj_kernel(h_ref, g_ref, w_ref, q_ref, k_ref, v_ref):
    tm = h_ref.shape[0]
    xn = _rmsnorm(h_ref[...], g_ref[...]).astype(BF16)
    qkv = jnp.dot(xn, w_ref[...], preferred_element_type=F32)
    kv_w = SWA_KV_HEADS * HEAD_DIM
    k_ref[...] = qkv[:, D_MODEL:D_MODEL + kv_w].astype(BF16)
    v_ref[...] = qkv[:, D_MODEL + kv_w:].astype(BF16)
    lane = lax.broadcasted_iota(jnp.int32, (tm, LANES), 1)
    for head in range(N_HEADS):
        kv = head // SWA_GROUP
        src = (head // HEADS_PER_LANE_BLOCK) * LANES
        q = qkv[:, src:src + LANES] * (QK_SCALE * LOG2E)
        if head % HEADS_PER_LANE_BLOCK != kv:
            q = pltpu.roll(q, HEAD_DIM, axis=1)
        q_ref[:, head * LANES:(head + 1) * LANES] = jnp.where(
            _lane_half_mask(lane, kv), q, 0.0).astype(BF16)


def _swa_proj(h, g, w_in):
    t, d = h.shape
    kv_w = SWA_KV_HEADS * HEAD_DIM
    tm = TOKEN_TILE
    return pl.pallas_call(
        _swa_proj_kernel,
        grid=(t // tm,),
        in_specs=[pl.BlockSpec((tm, d), lambda i: (i, 0)), _resident((1, d)),
                  _resident(w_in.shape)],
        out_specs=[pl.BlockSpec((tm, N_HEADS * LANES), lambda i: (i, 0)),
                   pl.BlockSpec((tm, kv_w), lambda i: (i, 0)),
                   pl.BlockSpec((tm, kv_w), lambda i: (i, 0))],
        out_shape=[jax.ShapeDtypeStruct((t, N_HEADS * LANES), BF16),
                   jax.ShapeDtypeStruct((t, kv_w), BF16),
                   jax.ShapeDtypeStruct((t, kv_w), BF16)],
        compiler_params=_params("parallel"),
        name="swa_proj",
    )(h, g.reshape(1, d), w_in.astype(BF16))


def _alibi_slope(head):
    return float(np.float32(2.0 ** (-8.0 * (head + 1) / N_HEADS)))


def _swa_attn_kernel(sinks_ref, q_ref, kp_ref, kc_ref, vp_ref, vc_ref, o_ref, bias_ref):
    w = WINDOW
    step = pl.program_id(1)

    @pl.when((pl.program_id(0) == 0) & (step == 0))
    def _():
        qi = lax.broadcasted_iota(jnp.int32, (w, 2 * w), 0)
        kj = lax.broadcasted_iota(jnp.int32, (w, 2 * w), 1)
        dist = qi + w - kj
        valid = (dist >= 0) & (dist < w)
        dist_f = dist.astype(F32)
        for head in range(N_HEADS):
            bias = jnp.where(valid, -_alibi_slope(head) * dist_f * LOG2E, NEG_INF)
            bias_ref[1, head] = bias
            bias_ref[0, head] = jnp.where(kj >= w, bias, NEG_INF)

    lane = lax.broadcasted_iota(jnp.int32, (w, LANES), 1)
    k_all = jnp.concatenate([kp_ref[0], kc_ref[0]], axis=0)
    v_all = jnp.concatenate([vp_ref[0], vc_ref[0]], axis=0)
    v_lane = lax.broadcasted_iota(jnp.int32, v_all.shape, 1)
    v_swapped = pltpu.roll(v_all, HEAD_DIM, axis=1)
    ones = jnp.ones_like(v_all)
    v_aug = []
    for kv in range(SWA_KV_HEADS):
        v_lo, v_hi = (v_all, v_swapped) if kv == 0 else (v_swapped, v_all)
        v_aug.append(jnp.concatenate([jnp.where(v_lane < HEAD_DIM, v_lo, ones),
                                      jnp.where(v_lane < HEAD_DIM, ones, v_hi)], axis=1))

    n_h = SWA_HEADS_PER_JOB
    jobs = [(i, h0) for i in range(SWA_BLOCKS_PER_STEP) for h0 in range(0, N_HEADS, n_h)]

    def scores(job):
        i, h0 = job
        qx = jnp.concatenate(
            [q_ref[0, i * w:(i + 1) * w, head * LANES:(head + 1) * LANES]
             for head in range(h0, h0 + n_h)], axis=0)
        return lax.dot_general(qx, k_all[i * w:(i + 2) * w], (((1,), (1,)), ((), ())),
                               preferred_element_type=F32)

    def absorb(job, s_all):
        i, h0 = job
        kv = h0 // SWA_GROUP
        table = jnp.minimum(step, 1) if i == 0 else 1
        p_rows, sink_p = [], []
        for g in range(n_h):
            sink = sinks_ref[h0 + g] * LOG2E
            s = s_all[g * w:(g + 1) * w] + bias_ref[table, h0 + g]
            m = jnp.maximum(jnp.max(s, axis=-1, keepdims=True), sink)
            p_rows.append(jnp.exp2(s - m).astype(BF16))
            sink_p.append(jnp.exp2(sink - m))
        acc = jnp.dot(jnp.concatenate(p_rows, axis=0), v_aug[kv][i * w:(i + 2) * w],
                      preferred_element_type=F32)
        for pair in range(n_h // HEADS_PER_LANE_BLOCK):
            num, den = [], []
            for d in range(HEADS_PER_LANE_BLOCK):
                g = pair * HEADS_PER_LANE_BLOCK + d
                a = acc[g * w:(g + 1) * w]
                num.append(a[:, d * LANES:(d + 1) * LANES])
                den.append(a[:, (1 - d) * LANES:(2 - d) * LANES] + sink_p[g])
            blk = h0 // HEADS_PER_LANE_BLOCK + pair
            o_ref[0, i * w:(i + 1) * w, blk * LANES:(blk + 1) * LANES] = (
                jnp.where(lane < HEAD_DIM, num[0], num[1])
                / jnp.where(lane < HEAD_DIM, den[0], den[1])).astype(BF16)

    pending = []
    for job in jobs:
        pending.append((job, scores(job)))
        if len(pending) > SWA_SCORES_AHEAD:
            absorb(*pending.pop(0))
    for item in pending:
        absorb(*item)


def _swa_attn(q, k, v, sinks):
    b, s, _ = q.shape
    w = WINDOW
    n = SWA_BLOCKS_PER_STEP
    kv_w = SWA_KV_HEADS * HEAD_DIM
    prev = pl.BlockSpec((1, w, kv_w), lambda i, j, *_: (i, jnp.maximum(j * n - 1, 0), 0))
    cur = pl.BlockSpec((1, n * w, kv_w), lambda i, j, *_: (i, j, 0))
    grid_spec = pltpu.PrefetchScalarGridSpec(
        num_scalar_prefetch=1,
        grid=(b, s // (n * w)),
        in_specs=[pl.BlockSpec((1, n * w, N_HEADS * LANES), lambda i, j, *_: (i, j, 0)),
                  prev, cur, prev, cur],
        out_specs=pl.BlockSpec((1, n * w, D_MODEL), lambda i, j, *_: (i, j, 0)),
        scratch_shapes=[pltpu.VMEM((2, N_HEADS, w, 2 * w), F32)],
    )
    return pl.pallas_call(
        _swa_attn_kernel,
        grid_spec=grid_spec,
        out_shape=jax.ShapeDtypeStruct((b, s, D_MODEL), BF16),
        compiler_params=_params("arbitrary", "arbitrary"),
        name="swa_attn",
    )(sinks.astype(F32), q, k, k, v, v)


def kernel(x, l0_ffn1_norm, l0_ffn1_w_gate, l0_ffn1_w_up, l0_ffn1_w_down, l0_mix_norm, l0_fox_w_in, l0_fox_b_forget, l0_fox_w_out, l0_ffn2_norm, l0_ffn2_w_gate, l0_ffn2_w_up, l0_ffn2_w_down, l1_ffn1_norm, l1_ffn1_w_gate, l1_ffn1_w_up, l1_ffn1_w_down, l1_mix_norm, l1_swa_w_in, l1_swa_sinks, l1_swa_w_out, l1_ffn2_norm, l1_ffn2_w_gate, l1_ffn2_w_up, l1_ffn2_w_down, final_norm):
    b, s, d = x.shape
    t = b * s
    h = x.reshape(t, d)

    def whole(*weights):
        return [(w, w.shape[1]) for w in weights]


    h, (fox_w_qkv, fox_w_out, wg, wu, wd) = _ffn(
        h, l0_ffn1_norm, l0_ffn1_w_gate, l0_ffn1_w_up, l0_ffn1_w_down,
        casts=[(l0_fox_w_in, 3 * d)] + whole(l0_fox_w_out, l0_ffn2_w_gate, l0_ffn2_w_up,
                                             l0_ffn2_w_down))
    q, k, v = _fox_proj(h.reshape(b, s, d), l0_mix_norm, l0_fox_w_in, fox_w_qkv,
                        l0_fox_b_forget)
    o = _fox_attn(q, k, v)
    h, (wg, wu, wd) = _ffn(h, l0_ffn2_norm, wg, wu, wd, attn=o.reshape(t, d), w_out=fox_w_out,
                           casts=whole(l1_ffn1_w_gate, l1_ffn1_w_up, l1_ffn1_w_down))

    h, (swa_w_in, swa_w_out, wg2, wu2, wd2) = _ffn(
        h, l1_ffn1_norm, wg, wu, wd,
        casts=whole(l1_swa_w_in, l1_swa_w_out, l1_ffn2_w_gate, l1_ffn2_w_up, l1_ffn2_w_down))
    q, k, v = _swa_proj(h, l1_mix_norm, swa_w_in)
    kv_w = SWA_KV_HEADS * HEAD_DIM
    o = _swa_attn(q.reshape(b, s, N_HEADS * LANES), k.reshape(b, s, kv_w),
                  v.reshape(b, s, kv_w), l1_swa_sinks)
    h, _ = _ffn(h, l1_ffn2_norm, wg2, wu2, wd2, attn=o.reshape(t, d), w_out=swa_w_out,
                final_g=final_norm)
    return h.reshape(b, s, d)
```

```python
import functools

import jax
import jax.numpy as jnp
import numpy as np
from jax import lax
from jax.experimental import pallas as pl
from jax.experimental.pallas import tpu as pltpu

F32 = jnp.float32
BF16 = jnp.bfloat16

D_MODEL = 1024
HEAD_DIM = 64
N_HEADS = D_MODEL // HEAD_DIM
SWA_KV_HEADS = 2
SWA_GROUP = N_HEADS // SWA_KV_HEADS
WINDOW = 128
RMS_EPS = 1e-6
NEG_INF = -1e30
QK_SCALE = HEAD_DIM ** -0.5

LANES = 128
BF16_TILE_ROWS = 16
CAST_BLOCK_ROWS = 128
HEADS_PER_LANE_BLOCK = LANES // HEAD_DIM
VMEM_LIMIT_BYTES = 56 * 1024 * 1024

TOKEN_TILE = 512
FFN_TILE = 1024
FFN_CHUNK = 512
FOX_TILE = 512
FOX_SCORES_AHEAD = 2
SWA_PROJ_TILE = 1024
SWA_BLOCKS_PER_STEP = 16
SWA_HEADS_PER_JOB = 4
SWA_SCORES_AHEAD = 1


def _rmsnorm(x, g):
    ms = jnp.mean(x * x, axis=-1, keepdims=True)
    return x * lax.rsqrt(ms + RMS_EPS) * g


def _resident(shape):
    zeros = (0,) * len(shape)
    return pl.BlockSpec(shape, lambda *_: zeros, pipeline_mode=pl.Buffered(1))


def _params(*semantics):
    return pltpu.CompilerParams(dimension_semantics=semantics,
                                vmem_limit_bytes=VMEM_LIMIT_BYTES)


def _ffn_kernel(*refs, d_ff, mixer_out, final_norm, n_casts):
    refs = list(refs)
    h_ref = refs.pop(0)
    attn_ref, wo_ref = (refs.pop(0), refs.pop(0)) if mixer_out else (None, None)
    g_ref, wg_ref, wu_ref, wd_ref = (refs.pop(0) for _ in range(4))
    gf_ref = refs.pop(0) if final_norm else None
    cast_in = [refs.pop(0) for _ in range(n_casts)]
    o_ref = refs.pop(0)
    cast_out = [refs.pop(0) for _ in range(n_casts)]
    a_ref, = refs
    for src, dst in zip(cast_in, cast_out):
        dst[...] = src[...].astype(BF16)
    x = h_ref[...]
    if mixer_out:
        x = x + jnp.dot(attn_ref[...], wo_ref[...], preferred_element_type=F32)
    xn = _rmsnorm(x, g_ref[...]).astype(BF16)
    for c0 in range(0, d_ff, FFN_CHUNK):
        cw = min(FFN_CHUNK, d_ff - c0)
        gate = jnp.dot(xn, wg_ref[:, c0:c0 + cw], preferred_element_type=F32)
        up = jnp.dot(xn, wu_ref[:, c0:c0 + cw], preferred_element_type=F32)
        a_ref[:, c0:c0 + cw] = (gate * jax.nn.sigmoid(gate) * up).astype(BF16)
    y = jnp.dot(a_ref[...], wd_ref[...], preferred_element_type=F32)
    out = x + 0.5 * y
    if final_norm:
        out = _rmsnorm(out, gf_ref[...])
    o_ref[...] = out


def _cast_spec(rows, cols, n_steps):
    block = rows // n_steps
    if rows % n_steps or block % BF16_TILE_ROWS:
        block = CAST_BLOCK_ROWS
    last = rows // block - 1
    return pl.BlockSpec((block, cols), lambda i: (jnp.minimum(i, last), 0))


def _ffn(h, g, wg, wu, wd, attn=None, w_out=None, final_g=None, casts=()):
    t, d = h.shape
    d_ff = wg.shape[1]
    n_steps = t // FFN_TILE
    tile = pl.BlockSpec((FFN_TILE, d), lambda i: (i, 0))
    in_specs, args = [tile], [h]
    if attn is not None:
        in_specs += [pl.BlockSpec((FFN_TILE, attn.shape[1]), lambda i: (i, 0)),
                     _resident(w_out.shape)]
        args += [attn, w_out.astype(BF16)]
    in_specs += [_resident((1, d)), _resident((d, d_ff)), _resident((d, d_ff)),
                 _resident((d_ff, d))]
    args += [g.reshape(1, d), wg.astype(BF16), wu.astype(BF16), wd.astype(BF16)]
    if final_g is not None:
        in_specs.append(_resident((1, d)))
        args.append(final_g.reshape(1, d))
    cast_specs = [_cast_spec(w.shape[0], cols, n_steps) for w, cols in casts]
    outs = pl.pallas_call(
        functools.partial(_ffn_kernel, d_ff=d_ff, mixer_out=attn is not None,
                          final_norm=final_g is not None, n_casts=len(casts)),
        grid=(n_steps,),
        in_specs=in_specs + cast_specs,
        out_specs=[tile] + cast_specs,
        out_shape=[jax.ShapeDtypeStruct((t, d), F32)]
        + [jax.ShapeDtypeStruct((w.shape[0], cols), BF16) for w, cols in casts],
        scratch_shapes=[pltpu.VMEM((FFN_TILE, d_ff), BF16)],
        compiler_params=_params("arbitrary"),
        name="ffn",
    )(*args, *(w for w, _ in casts))
    return outs[0], outs[1:]


def _split_bf16x3(x):
    hi = x.astype(BF16)
    r = x - hi.astype(F32)
    mid = r.astype(BF16)
    lo = (r - mid.astype(F32)).astype(BF16)
    return hi, mid, lo


AUG_PARTS = 3
LOG2E = float(np.log2(np.e))


def _data_half(head):
    return head % HEADS_PER_LANE_BLOCK


def _aug_base(head):
    return HEAD_DIM * (1 - _data_half(head))


def _fox_aug_tables():
    width = N_HEADS * LANES
    place = np.zeros((LANES, 2 * width), np.float32)
    const = np.zeros((1, 2 * width), np.float32)
    for head in range(N_HEADS):
        base = head * LANES + _aug_base(head)
        for part in range(AUG_PARTS):
            place[part * N_HEADS + head, base + part] = 1.0
            place[part * N_HEADS + head, width + base + AUG_PARTS + part] = -1.0
            const[0, base + AUG_PARTS + part] = 1.0
            const[0, width + base + part] = 1.0
    return jnp.asarray(place, BF16), jnp.asarray(const, F32)


def _fox_proj_kernel(h_ref, g_ref, wqkv_ref, wvt_ref, wf_ref, bf_ref, place_ref, const_ref,
                     q_ref, k_ref, vt_ref, carry_ref):
    @pl.when(pl.program_id(1) == 0)
    def _():
        carry_ref[...] = jnp.zeros_like(carry_ref)

    tm = h_ref.shape[1]
    width = N_HEADS * LANES
    xn = _rmsnorm(h_ref[0], g_ref[...]).astype(BF16)
    lane = lax.broadcasted_iota(jnp.int32, (tm, LANES), 1)

    def project(which):
        return jnp.dot(xn, wqkv_ref[:, which * D_MODEL:(which + 1) * D_MODEL],
                       preferred_element_type=F32)

    def head_blocks(proj, fill):
        for head in range(N_HEADS):
            src = (head // HEADS_PER_LANE_BLOCK) * LANES
            dst = slice(head * LANES, (head + 1) * LANES)
            yield dst, jnp.where(_lane_half_mask(lane, _data_half(head)),
                                 proj[:, src:src + LANES], fill(head, dst)).astype(BF16)

    def by_part(parts):
        out = jnp.zeros((tm, LANES), F32)
        for p in reversed(range(AUG_PARTS)):
            out = jnp.where(lane < (p + 1) * N_HEADS, parts[p], out)
        return out.astype(BF16)


    f_logit = jnp.dot(xn, wf_ref[...], preferred_element_type=F32) + bf_ref[...]
    v_t = lax.dot_general(wvt_ref[...], xn, (((1,), (1,)), ((), ())),
                          preferred_element_type=F32)
    log_f = jnp.minimum(f_logit, 0.0) - jnp.log1p(jnp.exp(-jnp.abs(f_logit)))
    parts = by_part([p.astype(F32) for p in _split_bf16x3(log_f)])
    t_out = lax.broadcasted_iota(jnp.int32, (tm, tm), 0)
    t_in = lax.broadcasted_iota(jnp.int32, (tm, tm), 1)
    tri = (t_in <= t_out).astype(BF16)
    sums = jnp.dot(tri, parts, preferred_element_type=F32)
    q_proj = project(0) * (QK_SCALE * LOG2E)
    ones_row = (lax.broadcasted_iota(jnp.int32, (LANES - HEAD_DIM, tm), 0) == 0).astype(BF16)
    for head in range(N_HEADS):
        vt_ref[0, head * LANES:head * LANES + HEAD_DIM, :] = (
            v_t[head * HEAD_DIM:(head + 1) * HEAD_DIM].astype(BF16))
        vt_ref[0, head * LANES + HEAD_DIM:(head + 1) * LANES, :] = ones_row

    c = sums + carry_ref[...]
    for p in range(1, AUG_PARTS):
        c = c + pltpu.roll(sums, LANES - p * N_HEADS, axis=1)
    carry_ref[...] = c[tm - 1:tm, :]
    c_parts = [p.astype(F32) for p in _split_bf16x3(c * LOG2E)]
    c_parts = by_part([c_parts[0]] + [pltpu.roll(c_parts[p], p * N_HEADS, axis=1)
                                      for p in range(1, AUG_PARTS)])
    aug = jnp.dot(c_parts, place_ref[...], preferred_element_type=F32) + const_ref[...]
    k_proj = project(1)
    for dst, blk in head_blocks(q_proj, lambda head, dst: aug[:, dst]):
        q_ref[0, :, dst] = blk
    for dst, blk in head_blocks(k_proj, lambda head, dst: aug[:, width + dst.start:
                                                               width + dst.stop]):
        k_ref[0, :, dst] = blk


def _fox_proj(h, g, w_in, w_qkv, b_forget):
    b, s, d = h.shape
    pad = LANES - AUG_PARTS * N_HEADS
    w_f = jnp.pad(jnp.tile(w_in[:, 3 * d:], (1, AUG_PARTS)), ((0, 0), (0, pad))).astype(BF16)
    b_f = jnp.pad(jnp.tile(b_forget.astype(F32), AUG_PARTS), (0, pad)).reshape(1, LANES)
    w_vt = w_qkv[:, 2 * d:].T
    place, const = _fox_aug_tables()
    tm = TOKEN_TILE
    width = N_HEADS * LANES
    out_spec = pl.BlockSpec((1, tm, width), lambda i, j: (i, j, 0))
    out_shape = jax.ShapeDtypeStruct((b, s, width), BF16)
    return pl.pallas_call(
        _fox_proj_kernel,
        grid=(b, s // tm),
        in_specs=[pl.BlockSpec((1, tm, d), lambda i, j: (i, j, 0)),
                  _resident((1, d)), _resident(w_qkv.shape), _resident(w_vt.shape),
                  _resident(w_f.shape), _resident(b_f.shape), _resident(place.shape),
                  _resident(const.shape)],
        out_specs=[out_spec, out_spec, pl.BlockSpec((1, width, tm), lambda i, j: (i, 0, j))],
        out_shape=[out_shape, out_shape, jax.ShapeDtypeStruct((b, width, s), BF16)],
        scratch_shapes=[pltpu.VMEM((1, LANES), F32)],
        compiler_params=_params("parallel", "arbitrary"),
        name="fox_proj",
    )(h, g.reshape(1, d), w_qkv, w_vt, w_f, b_f, place, const)


def _fox_attn_kernel(q_ref, k_ref, vt_ref, o_ref):
    seq = q_ref.shape[1]
    t = FOX_TILE
    half = t // 2
    heads = range(HEADS_PER_LANE_BLOCK)
    first_mask = (lax.broadcasted_iota(jnp.int32, (half, half), 0)
                  <= lax.broadcasted_iota(jnp.int32, (half, half), 1))
    second_mask = (lax.broadcasted_iota(jnp.int32, (t, half), 0)
                   <= lax.broadcasted_iota(jnp.int32, (t, half), 1) + half)

    cols = [slice(hh * LANES, (hh + 1) * LANES) for hh in heads]

    jobs = []
    for qi in range(seq // t):
        q0 = qi * t
        for j in range(qi):
            jobs += [(qi, hh, (q0, t), (j * t, t), None) for hh in heads]
        for hh in heads:
            jobs.append((qi, hh, (q0, half), (q0, half), first_mask))
            jobs.append((qi, hh, (q0 + half, half), (q0, t), second_mask))

    state = {}
    finished = {}

    def scores(job):
        _, hh, (r0, rn), (k0, kn), mask = job
        s = lax.dot_general(k_ref[0, k0:k0 + kn, cols[hh]], q_ref[0, r0:r0 + rn, cols[hh]],
                            (((1,), (1,)), ((), ())), preferred_element_type=F32)
        return s if mask is None else jnp.where(mask, s, NEG_INF)

    def absorb(job, s):
        qi, hh, (r0, rn), (k0, kn), mask = job
        m, acc = state.get((qi, hh), (jnp.full((1, t), NEG_INF, F32),
                                      jnp.zeros((LANES, t), F32)))
        if mask is not None:
            lo = r0 - qi * t
            m, acc = m[:, lo:lo + rn], acc[:, lo:lo + rn]
        m_new = jnp.maximum(m, jnp.max(s, axis=0, keepdims=True))
        p = jnp.exp2(s - m_new).astype(BF16)
        acc = jnp.exp2(m - m_new) * acc + jnp.dot(vt_ref[0, cols[hh], k0:k0 + kn], p,
                                                  preferred_element_type=F32)
        if mask is None:
            state[(qi, hh)] = (m_new, acc)
            return
        accs = finished.setdefault(r0, {})
        accs[hh] = acc
        if len(accs) == len(heads):
            o_t = jnp.concatenate([accs[h][:HEAD_DIM] / accs[h][HEAD_DIM:HEAD_DIM + 1]
                                   for h in heads], axis=0)
            o_ref[0, r0:r0 + rn, :] = o_t.T.astype(BF16)

    pending = []
    for job in jobs:
        pending.append((job, scores(job)))
        if len(pending) > FOX_SCORES_AHEAD:
            absorb(*pending.pop(0))
    for item in pending:
        absorb(*item)


def _fox_attn(q, k, v_t):
    b, s, _ = q.shape
    pair_w = HEADS_PER_LANE_BLOCK * LANES
    blk = pl.BlockSpec((1, s, pair_w), lambda i, p: (i, 0, p))
    return pl.pallas_call(
        _fox_attn_kernel,
        grid=(b, N_HEADS // HEADS_PER_LANE_BLOCK),
        in_specs=[blk, blk, pl.BlockSpec((1, pair_w, s), lambda i, p: (i, p, 0))],
        out_specs=pl.BlockSpec((1, s, LANES), lambda i, p: (i, 0, p)),
        out_shape=jax.ShapeDtypeStruct((b, s, D_MODEL), BF16),
        compiler_params=_params("parallel", "parallel"),
        name="fox_attn",
    )(q, k, v_t)


def _lane_half_mask(lane, half):
    return (lane >= half * HEAD_DIM) & (lane < (half + 1) * HEAD_DIM)


def _swa_proj_kernel(h_ref, g_ref, w_ref, q_ref, k_ref, v_ref):
    tm = h_ref.shape[0]
    xn = _rmsnorm(h_ref[...], g_ref[...]).astype(BF16)
    qkv = jnp.dot(xn, w_ref[...], preferred_element_type=F32)
    kv_w = SWA_KV_HEADS * HEAD_DIM
    k_ref[...] = qkv[:, D_MODEL:D_MODEL + kv_w].astype(BF16)
    v_ref[...] = qkv[:, D_MODEL + kv_w:].astype(BF16)
    lane = lax.broadcasted_iota(jnp.int32, (tm, LANES), 1)
    for head in range(N_HEADS):
        kv = head // SWA_GROUP
        src = (head // HEADS_PER_LANE_BLOCK) * LANES
        q = qkv[:, src:src + LANES] * (QK_SCALE * LOG2E)
        if head % HEADS_PER_LANE_BLOCK != kv:
            q = pltpu.roll(q, HEAD_DIM, axis=1)
        q_ref[:, head * LANES:(head + 1) * LANES] = jnp.where(
            _lane_half_mask(lane, kv), q, 0.0).astype(BF16)


def _swa_proj(h, g, w_in):
    t, d = h.shape
    kv_w = SWA_KV_HEADS * HEAD_DIM
    tm = SWA_PROJ_TILE
    return pl.pallas_call(
        _swa_proj_kernel,
        grid=(t // tm,),
        in_specs=[pl.BlockSpec((tm, d), lambda i: (i, 0)), _resident((1, d)),
                  _resident(w_in.shape)],
        out_specs=[pl.BlockSpec((tm, N_HEADS * LANES), lambda i: (i, 0)),
                   pl.BlockSpec((tm, kv_w), lambda i: (i, 0)),
                   pl.BlockSpec((tm, kv_w), lambda i: (i, 0))],
        out_shape=[jax.ShapeDtypeStruct((t, N_HEADS * LANES), BF16),
                   jax.ShapeDtypeStruct((t, kv_w), BF16),
                   jax.ShapeDtypeStruct((t, kv_w), BF16)],
        compiler_params=_params("parallel"),
        name="swa_proj",
    )(h, g.reshape(1, d), w_in.astype(BF16))


def _alibi_slope(head):
    return float(np.float32(2.0 ** (-8.0 * (head + 1) / N_HEADS)))


def _swa_attn_kernel(sinks_ref, q_ref, kp_ref, kc_ref, vp_ref, vc_ref, o_ref, bias_ref):
    w = WINDOW
    step = pl.program_id(1)

    @pl.when((pl.program_id(0) == 0) & (step == 0))
    def _():
        qi = lax.broadcasted_iota(jnp.int32, (w, 2 * w), 0)
        kj = lax.broadcasted_iota(jnp.int32, (w, 2 * w), 1)
        dist = qi + w - kj
        valid = (dist >= 0) & (dist < w)
        dist_f = dist.astype(F32)
        for head in range(N_HEADS):
            bias = jnp.where(valid, -_alibi_slope(head) * dist_f * LOG2E, NEG_INF)
            bias_ref[1, head] = bias
            bias_ref[0, head] = jnp.where(kj >= w, bias, NEG_INF)

    lane = lax.broadcasted_iota(jnp.int32, (w, LANES), 1)
    k_all = jnp.concatenate([kp_ref[0], kc_ref[0]], axis=0)
    v_all = jnp.concatenate([vp_ref[0], vc_ref[0]], axis=0)
    v_lane = lax.broadcasted_iota(jnp.int32, v_all.shape, 1)
    v_swapped = pltpu.roll(v_all, HEAD_DIM, axis=1)
    ones = jnp.ones_like(v_all)
    v_aug = []
    for kv in range(SWA_KV_HEADS):
        v_lo, v_hi = (v_all, v_swapped) if kv == 0 else (v_swapped, v_all)
        v_aug.append(jnp.concatenate([jnp.where(v_lane < HEAD_DIM, v_lo, ones),
                                      jnp.where(v_lane < HEAD_DIM, ones, v_hi)], axis=1))

    n_h = SWA_HEADS_PER_JOB
    jobs = [(i, h0) for i in range(SWA_BLOCKS_PER_STEP) for h0 in range(0, N_HEADS, n_h)]

    def scores(job):
        i, h0 = job
        qx = jnp.concatenate(
            [q_ref[0, i * w:(i + 1) * w, head * LANES:(head + 1) * LANES]
             for head in range(h0, h0 + n_h)], axis=0)
        return lax.dot_general(qx, k_all[i * w:(i + 2) * w], (((1,), (1,)), ((), ())),
                               preferred_element_type=F32)

    def absorb(job, s_all):
        i, h0 = job
        kv = h0 // SWA_GROUP
        table = jnp.minimum(step, 1) if i == 0 else 1
        p_rows, sink_p = [], []
        for g in range(n_h):
            sink = sinks_ref[h0 + g] * LOG2E
            s = s_all[g * w:(g + 1) * w] + bias_ref[table, h0 + g]
            m = jnp.maximum(jnp.max(s, axis=-1, keepdims=True), sink)
            p_rows.append(jnp.exp2(s - m).astype(BF16))
            sink_p.append(jnp.exp2(sink - m))
        acc = jnp.dot(jnp.concatenate(p_rows, axis=0), v_aug[kv][i * w:(i + 2) * w],
                      preferred_element_type=F32)
        for pair in range(n_h // HEADS_PER_LANE_BLOCK):
            num, den = [], []
            for d in range(HEADS_PER_LANE_BLOCK):
                g = pair * HEADS_PER_LANE_BLOCK + d
                a = acc[g * w:(g + 1) * w]
                num.append(a[:, d * LANES:(d + 1) * LANES])
                den.append(a[:, (1 - d) * LANES:(2 - d) * LANES] + sink_p[g])
            blk = h0 // HEADS_PER_LANE_BLOCK + pair
            o_ref[0, i * w:(i + 1) * w, blk * LANES:(blk + 1) * LANES] = (
                jnp.where(lane < HEAD_DIM, num[0], num[1])
                / jnp.where(lane < HEAD_DIM, den[0], den[1])).astype(BF16)

    pending = []
    for job in jobs:
        pending.append((job, scores(job)))
        if len(pending) > SWA_SCORES_AHEAD:
            absorb(*pending.pop(0))
    for item in pending:
        absorb(*item)


def _swa_attn(q, k, v, sinks):
    b, s, _ = q.shape
    w = WINDOW
    n = SWA_BLOCKS_PER_STEP
    kv_w = SWA_KV_HEADS * HEAD_DIM
    prev = pl.BlockSpec((1, w, kv_w), lambda i, j, *_: (i, jnp.maximum(j * n - 1, 0), 0))
    cur = pl.BlockSpec((1, n * w, kv_w), lambda i, j, *_: (i, j, 0))
    grid_spec = pltpu.PrefetchScalarGridSpec(
        num_scalar_prefetch=1,
        grid=(b, s // (n * w)),
        in_specs=[pl.BlockSpec((1, n * w, N_HEADS * LANES), lambda i, j, *_: (i, j, 0)),
                  prev, cur, prev, cur],
        out_specs=pl.BlockSpec((1, n * w, D_MODEL), lambda i, j, *_: (i, j, 0)),
        scratch_shapes=[pltpu.VMEM((2, N_HEADS, w, 2 * w), F32)],
    )
    return pl.pallas_call(
        _swa_attn_kernel,
        grid_spec=grid_spec,
        out_shape=jax.ShapeDtypeStruct((b, s, D_MODEL), BF16),
        compiler_params=_params("arbitrary", "arbitrary"),
        name="swa_attn",
    )(sinks.astype(F32), q, k, k, v, v)


def kernel(x, l0_ffn1_norm, l0_ffn1_w_gate, l0_ffn1_w_up, l0_ffn1_w_down, l0_mix_norm, l0_fox_w_in, l0_fox_b_forget, l0_fox_w_out, l0_ffn2_norm, l0_ffn2_w_gate, l0_ffn2_w_up, l0_ffn2_w_down, l1_ffn1_norm, l1_ffn1_w_gate, l1_ffn1_w_up, l1_ffn1_w_down, l1_mix_norm, l1_swa_w_in, l1_swa_sinks, l1_swa_w_out, l1_ffn2_norm, l1_ffn2_w_gate, l1_ffn2_w_up, l1_ffn2_w_down, final_norm):
    b, s, d = x.shape
    t = b * s
    h = x.reshape(t, d)

    def whole(*weights):
        return [(w, w.shape[1]) for w in weights]


    h, (fox_w_qkv, fox_w_out, wg, wu, wd) = _ffn(
        h, l0_ffn1_norm, l0_ffn1_w_gate, l0_ffn1_w_up, l0_ffn1_w_down,
        casts=[(l0_fox_w_in, 3 * d)] + whole(l0_fox_w_out, l0_ffn2_w_gate, l0_ffn2_w_up,
                                             l0_ffn2_w_down))
    q, k, v = _fox_proj(h.reshape(b, s, d), l0_mix_norm, l0_fox_w_in, fox_w_qkv,
                        l0_fox_b_forget)
    o = _fox_attn(q, k, v)
    h, (wg, wu, wd) = _ffn(h, l0_ffn2_norm, wg, wu, wd, attn=o.reshape(t, d), w_out=fox_w_out,
                           casts=whole(l1_ffn1_w_gate, l1_ffn1_w_up, l1_ffn1_w_down))

    h, (swa_w_in, swa_w_out, wg2, wu2, wd2) = _ffn(
        h, l1_ffn1_norm, wg, wu, wd,
        casts=whole(l1_swa_w_in, l1_swa_w_out, l1_ffn2_w_gate, l1_ffn2_w_up, l1_ffn2_w_down))
    q, k, v = _swa_proj(h, l1_mix_norm, swa_w_in)
    kv_w = SWA_KV_HEADS * HEAD_DIM
    o = _swa_attn(q.reshape(b, s, N_HEADS * LANES), k.reshape(b, s, kv_w),
                  v.reshape(b, s, kv_w), l1_swa_sinks)
    h, _ = _ffn(h, l1_ffn2_norm, wg2, wu2, wd2, attn=o.reshape(t, d), w_out=swa_w_out,
                final_g=final_norm)
    return h.reshape(b, s, d)
```

```python
import functools

import jax
import jax.numpy as jnp
import numpy as np
from jax import lax
from jax.experimental import pallas as pl
from jax.experimental.pallas import tpu as pltpu

F32 = jnp.float32
BF16 = jnp.bfloat16

D_MODEL = 1024
HEAD_DIM = 64
N_HEADS = D_MODEL // HEAD_DIM
SWA_KV_HEADS = 2
SWA_GROUP = N_HEADS // SWA_KV_HEADS
WINDOW = 128
RMS_EPS = 1e-6
NEG_INF = -1e30
QK_SCALE = HEAD_DIM ** -0.5

LANES = 128
BF16_TILE_ROWS = 16
CAST_BLOCK_ROWS = 128
HEADS_PER_LANE_BLOCK = LANES // HEAD_DIM
VMEM_LIMIT_BYTES = 56 * 1024 * 1024

TOKEN_TILE = 512
FFN_TILE = 1024
FFN_CHUNK = 512
FOX_TILE = 512
FOX_SCORES_AHEAD = 4
SWA_PROJ_TILE = 1024
SWA_BLOCKS_PER_STEP = 16
SWA_HEADS_PER_JOB = 4
SWA_SCORES_AHEAD = 1


def _rmsnorm(x, g):
    ms = jnp.mean(x * x, axis=-1, keepdims=True)
    return x * lax.rsqrt(ms + RMS_EPS) * g


def _resident(shape):
    zeros = (0,) * len(shape)
    return pl.BlockSpec(shape, lambda *_: zeros, pipeline_mode=pl.Buffered(1))


def _params(*semantics):
    return pltpu.CompilerParams(dimension_semantics=semantics,
                                vmem_limit_bytes=VMEM_LIMIT_BYTES)


def _ffn_kernel(*refs, d_ff, mixer_out, final_norm, n_casts):
    refs = list(refs)
    h_ref = refs.pop(0)
    attn_ref, wo_ref = (refs.pop(0), refs.pop(0)) if mixer_out else (None, None)
    g_ref, wg_ref, wu_ref, wd_ref = (refs.pop(0) for _ in range(4))
    gf_ref = refs.pop(0) if final_norm else None
    cast_in = [refs.pop(0) for _ in range(n_casts)]
    o_ref = refs.pop(0)
    cast_out = [refs.pop(0) for _ in range(n_casts)]
    a_ref, = refs
    for src, dst in zip(cast_in, cast_out):
        dst[...] = src[...].astype(BF16)
    x = h_ref[...]
    if mixer_out:
        x = x + jnp.dot(attn_ref[...], wo_ref[...], preferred_element_type=F32)
    xn = _rmsnorm(x, g_ref[...]).astype(BF16)
    for c0 in range(0, d_ff, FFN_CHUNK):
        cw = min(FFN_CHUNK, d_ff - c0)
        gate = jnp.dot(xn, wg_ref[:, c0:c0 + cw], preferred_element_type=F32)
        up = jnp.dot(xn, wu_ref[:, c0:c0 + cw], preferred_element_type=F32)
        a_ref[:, c0:c0 + cw] = (gate * jax.nn.sigmoid(gate) * up).astype(BF16)
    y = jnp.dot(a_ref[...], wd_ref[...], preferred_element_type=F32)
    out = x + 0.5 * y
    if final_norm:
        out = _rmsnorm(out, gf_ref[...])
    o_ref[...] = out


def _cast_spec(rows, cols, n_steps):
    block = rows // n_steps
    if rows % n_steps or block % BF16_TILE_ROWS:
        block = CAST_BLOCK_ROWS
    last = rows // block - 1
    return pl.BlockSpec((block, cols), lambda i: (jnp.minimum(i, last), 0))


def _ffn(h, g, wg, wu, wd, attn=None, w_out=None, final_g=None, casts=()):
    t, d = h.shape
    d_ff = wg.shape[1]
    n_steps = t // FFN_TILE
    tile = pl.BlockSpec((FFN_TILE, d), lambda i: (i, 0))
    in_specs, args = [tile], [h]
    if attn is not None:
        in_specs += [pl.BlockSpec((FFN_TILE, attn.shape[1]), lambda i: (i, 0)),
                     _resident(w_out.shape)]
        args += [attn, w_out.astype(BF16)]
    in_specs += [_resident((1, d)), _resident((d, d_ff)), _resident((d, d_ff)),
                 _resident((d_ff, d))]
    args += [g.reshape(1, d), wg.astype(BF16), wu.astype(BF16), wd.astype(BF16)]
    if final_g is not None:
        in_specs.append(_resident((1, d)))
        args.append(final_g.reshape(1, d))
    cast_specs = [_cast_spec(w.shape[0], cols, n_steps) for w, cols in casts]
    outs = pl.pallas_call(
        functools.partial(_ffn_kernel, d_ff=d_ff, mixer_out=attn is not None,
                          final_norm=final_g is not None, n_casts=len(casts)),
        grid=(n_steps,),
        in_specs=in_specs + cast_specs,
        out_specs=[tile] + cast_specs,
        out_shape=[jax.ShapeDtypeStruct((t, d), F32)]
        + [jax.ShapeDtypeStruct((w.shape[0], cols), BF16) for w, cols in casts],
        scratch_shapes=[pltpu.VMEM((FFN_TILE, d_ff), BF16)],
        compiler_params=_params("arbitrary"),
        name="ffn",
    )(*args, *(w for w, _ in casts))
    return outs[0], outs[1:]


def _split_bf16x3(x):
    hi = x.astype(BF16)
    r = x - hi.astype(F32)
    mid = r.astype(BF16)
    lo = (r - mid.astype(F32)).astype(BF16)
    return hi, mid, lo


AUG_PARTS = 3
LOG2E = float(np.log2(np.e))


def _data_half(head):
    return head % HEADS_PER_LANE_BLOCK


def _aug_base(head):
    return HEAD_DIM * (1 - _data_half(head))


def _fox_aug_tables():
    width = N_HEADS * LANES
    place = np.zeros((LANES, width), np.float32)
    const = np.zeros((1, width), np.float32)
    for head in range(N_HEADS):
        q_base = head * LANES + _aug_base(head)
        k_base = head * LANES + (_aug_base(head) + HEAD_DIM) % LANES
        for part in range(AUG_PARTS):
            place[part * N_HEADS + head, q_base + part] = 1.0
            place[part * N_HEADS + head, k_base + AUG_PARTS + part] = -1.0
            const[0, q_base + AUG_PARTS + part] = 1.0
            const[0, k_base + part] = 1.0
    return jnp.asarray(place, BF16), jnp.asarray(const, F32)


def _fox_proj_kernel(h_ref, g_ref, wqkv_ref, wvt_ref, wf_ref, bf_ref, place_ref, const_ref,
                     q_ref, k_ref, vt_ref, carry_ref):
    @pl.when(pl.program_id(1) == 0)
    def _():
        carry_ref[...] = jnp.zeros_like(carry_ref)

    tm = h_ref.shape[1]
    xn = _rmsnorm(h_ref[0], g_ref[...]).astype(BF16)
    lane = lax.broadcasted_iota(jnp.int32, (tm, LANES), 1)

    def project(which):
        return jnp.dot(xn, wqkv_ref[:, which * D_MODEL:(which + 1) * D_MODEL],
                       preferred_element_type=F32)

    def head_blocks(proj, fill):
        for head in range(N_HEADS):
            src = (head // HEADS_PER_LANE_BLOCK) * LANES
            dst = slice(head * LANES, (head + 1) * LANES)
            yield dst, jnp.where(_lane_half_mask(lane, _data_half(head)),
                                 proj[:, src:src + LANES], fill(head, dst)).astype(BF16)

    def by_part(parts):
        out = jnp.zeros((tm, LANES), F32)
        for p in reversed(range(AUG_PARTS)):
            out = jnp.where(lane < (p + 1) * N_HEADS, parts[p], out)
        return out.astype(BF16)


    f_logit = jnp.dot(xn, wf_ref[...], preferred_element_type=F32) + bf_ref[...]
    v_t = lax.dot_general(wvt_ref[...], xn, (((1,), (1,)), ((), ())),
                          preferred_element_type=F32)
    log_f = jnp.minimum(f_logit, 0.0) - jnp.log1p(jnp.exp(-jnp.abs(f_logit)))
    parts = by_part([p.astype(F32) for p in _split_bf16x3(log_f)])
    t_out = lax.broadcasted_iota(jnp.int32, (tm, tm), 0)
    t_in = lax.broadcasted_iota(jnp.int32, (tm, tm), 1)
    tri = (t_in <= t_out).astype(BF16)
    sums = jnp.dot(tri, parts, preferred_element_type=F32)
    q_proj = project(0) * (QK_SCALE * LOG2E)
    ones_row = (lax.broadcasted_iota(jnp.int32, (LANES - HEAD_DIM, tm), 0) == 0).astype(BF16)
    for head in range(N_HEADS):
        vt_ref[0, head * LANES:head * LANES + HEAD_DIM, :] = (
            v_t[head * HEAD_DIM:(head + 1) * HEAD_DIM].astype(BF16))
        vt_ref[0, head * LANES + HEAD_DIM:(head + 1) * LANES, :] = ones_row

    c = sums + carry_ref[...]
    for p in range(1, AUG_PARTS):
        c = c + pltpu.roll(sums, LANES - p * N_HEADS, axis=1)
    carry_ref[...] = c[tm - 1:tm, :]
    c_parts = [p.astype(F32) for p in _split_bf16x3(c * LOG2E)]
    c_parts = by_part([c_parts[0]] + [pltpu.roll(c_parts[p], p * N_HEADS, axis=1)
                                      for p in range(1, AUG_PARTS)])
    aug = jnp.dot(c_parts, place_ref[...], preferred_element_type=F32) + const_ref[...]
    k_proj = project(1)
    for dst, blk in head_blocks(q_proj, lambda head, dst: aug[:, dst]):
        q_ref[0, :, dst] = blk
    for dst, blk in head_blocks(k_proj, lambda head, dst: pltpu.roll(aug[:, dst], HEAD_DIM,
                                                                     axis=1)):
        k_ref[0, :, dst] = blk


def _fox_proj(h, g, w_in, w_qkv, b_forget):
    b, s, d = h.shape
    pad = LANES - AUG_PARTS * N_HEADS
    w_f = jnp.pad(jnp.tile(w_in[:, 3 * d:], (1, AUG_PARTS)), ((0, 0), (0, pad))).astype(BF16)
    b_f = jnp.pad(jnp.tile(b_forget.astype(F32), AUG_PARTS), (0, pad)).reshape(1, LANES)
    w_vt = w_qkv[:, 2 * d:].T
    place, const = _fox_aug_tables()
    tm = TOKEN_TILE
    width = N_HEADS * LANES
    out_spec = pl.BlockSpec((1, tm, width), lambda i, j: (i, j, 0))
    out_shape = jax.ShapeDtypeStruct((b, s, width), BF16)
    return pl.pallas_call(
        _fox_proj_kernel,
        grid=(b, s // tm),
        in_specs=[pl.BlockSpec((1, tm, d), lambda i, j: (i, j, 0)),
                  _resident((1, d)), _resident(w_qkv.shape), _resident(w_vt.shape),
                  _resident(w_f.shape), _resident(b_f.shape), _resident(place.shape),
                  _resident(const.shape)],
        out_specs=[out_spec, out_spec, pl.BlockSpec((1, width, tm), lambda i, j: (i, 0, j))],
        out_shape=[out_shape, out_shape, jax.ShapeDtypeStruct((b, width, s), BF16)],
        scratch_shapes=[pltpu.VMEM((1, LANES), F32)],
        compiler_params=_params("parallel", "arbitrary"),
        name="fox_proj",
    )(h, g.reshape(1, d), w_qkv, w_vt, w_f, b_f, place, const)


def _fox_attn_kernel(q_ref, k_ref, vt_ref, o_ref):
    seq = q_ref.shape[1]
    t = FOX_TILE
    half = t // 2
    heads = range(HEADS_PER_LANE_BLOCK)
    first_mask = (lax.broadcasted_iota(jnp.int32, (half, half), 0)
                  <= lax.broadcasted_iota(jnp.int32, (half, half), 1))
    second_mask = (lax.broadcasted_iota(jnp.int32, (t, half), 0)
                   <= lax.broadcasted_iota(jnp.int32, (t, half), 1) + half)

    cols = [slice(hh * LANES, (hh + 1) * LANES) for hh in heads]

    jobs = []
    for qi in range(seq // t):
        q0 = qi * t
        for j in range(qi):
            jobs += [(qi, hh, (q0, t), (j * t, t), None) for hh in heads]
        for hh in heads:
            jobs.append((qi, hh, (q0, half), (q0, half), first_mask))
            jobs.append((qi, hh, (q0 + half, half), (q0, t), second_mask))

    state = {}
    finished = {}

    def scores(job):
        _, hh, (r0, rn), (k0, kn), mask = job
        s = lax.dot_general(k_ref[0, k0:k0 + kn, cols[hh]], q_ref[0, r0:r0 + rn, cols[hh]],
                            (((1,), (1,)), ((), ())), preferred_element_type=F32)
        return s if mask is None else jnp.where(mask, s, NEG_INF)

    def absorb(job, s):
        qi, hh, (r0, rn), (k0, kn), mask = job
        m, acc = state.get((qi, hh), (jnp.full((1, t), NEG_INF, F32),
                                      jnp.zeros((LANES, t), F32)))
        if mask is not None:
            lo = r0 - qi * t
            m, acc = m[:, lo:lo + rn], acc[:, lo:lo + rn]
        m_new = jnp.maximum(m, jnp.max(s, axis=0, keepdims=True))
        p = jnp.exp2(s - m_new).astype(BF16)
        acc = jnp.exp2(m - m_new) * acc + jnp.dot(vt_ref[0, cols[hh], k0:k0 + kn], p,
                                                  preferred_element_type=F32)
        if mask is None:
            state[(qi, hh)] = (m_new, acc)
            return
        accs = finished.setdefault(r0, {})
        accs[hh] = acc
        if len(accs) == len(heads):
            o_t = jnp.concatenate([accs[h][:HEAD_DIM] / accs[h][HEAD_DIM:HEAD_DIM + 1]
                                   for h in heads], axis=0)
            o_ref[0, r0:r0 + rn, :] = o_t.T.astype(BF16)

    pending = []
    for job in jobs:
        pending.append((job, scores(job)))
        if len(pending) > FOX_SCORES_AHEAD:
            absorb(*pending.pop(0))
    for item in pending:
        absorb(*item)


def _fox_attn(q, k, v_t):
    b, s, _ = q.shape
    pair_w = HEADS_PER_LANE_BLOCK * LANES
    blk = pl.BlockSpec((1, s, pair_w), lambda i, p: (i, 0, p))
    return pl.pallas_call(
        _fox_attn_kernel,
        grid=(b, N_HEADS // HEADS_PER_LANE_BLOCK),
        in_specs=[blk, blk, pl.BlockSpec((1, pair_w, s), lambda i, p: (i, p, 0))],
        out_specs=pl.BlockSpec((1, s, LANES), lambda i, p: (i, 0, p)),
        out_shape=jax.ShapeDtypeStruct((b, s, D_MODEL), BF16),
        compiler_params=_params("parallel", "parallel"),
        name="fox_attn",
    )(q, k, v_t)


def _lane_half_mask(lane, half):
    return (lane >= half * HEAD_DIM) & (lane < (half + 1) * HEAD_DIM)


def _swa_proj_kernel(h_ref, g_ref, w_ref, q_ref, k_ref, v_ref):
    tm = h_ref.shape[0]
    xn = _rmsnorm(h_ref[...], g_ref[...]).astype(BF16)
    qkv = jnp.dot(xn, w_ref[...], preferred_element_type=F32)
    kv_w = SWA_KV_HEADS * HEAD_DIM
    k_ref[...] = qkv[:, D_MODEL:D_MODEL + kv_w].astype(BF16)
    v_ref[...] = qkv[:, D_MODEL + kv_w:].astype(BF16)
    lane = lax.broadcasted_iota(jnp.int32, (tm, LANES), 1)
    for head in range(N_HEADS):
        kv = head // SWA_GROUP
        src = (head // HEADS_PER_LANE_BLOCK) * LANES
        q = qkv[:, src:src + LANES] * (QK_SCALE * LOG2E)
        if head % HEADS_PER_LANE_BLOCK != kv:
            q = pltpu.roll(q, HEAD_DIM, axis=1)
        q_ref[:, head * LANES:(head + 1) * LANES] = jnp.where(
            _lane_half_mask(lane, kv), q, 0.0).astype(BF16)


def _swa_proj(h, g, w_in):
    t, d = h.shape
    kv_w = SWA_KV_HEADS * HEAD_DIM
    tm = SWA_PROJ_TILE
    return pl.pallas_call(
        _swa_proj_kernel,
        grid=(t // tm,),
        in_specs=[pl.BlockSpec((tm, d), lambda i: (i, 0)), _resident((1, d)),
                  _resident(w_in.shape)],
        out_specs=[pl.BlockSpec((tm, N_HEADS * LANES), lambda i: (i, 0)),
                   pl.BlockSpec((tm, kv_w), lambda i: (i, 0)),
                   pl.BlockSpec((tm, kv_w), lambda i: (i, 0))],
        out_shape=[jax.ShapeDtypeStruct((t, N_HEADS * LANES), BF16),
                   jax.ShapeDtypeStruct((t, kv_w), BF16),
                   jax.ShapeDtypeStruct((t, kv_w), BF16)],
        compiler_params=_params("parallel"),
        name="swa_proj",
    )(h, g.reshape(1, d), w_in.astype(BF16))


def _alibi_slope(head):
    return float(np.float32(2.0 ** (-8.0 * (head + 1) / N_HEADS)))


def _swa_attn_kernel(sinks_ref, q_ref, kp_ref, kc_ref, vp_ref, vc_ref, o_ref, bias_ref):
    w = WINDOW
    step = pl.program_id(1)

    @pl.when((pl.program_id(0) == 0) & (step == 0))
    def _():
        qi = lax.broadcasted_iota(jnp.int32, (w, 2 * w), 0)
        kj = lax.broadcasted_iota(jnp.int32, (w, 2 * w), 1)
        dist = qi + w - kj
        valid = (dist >= 0) & (dist < w)
        dist_f = dist.astype(F32)
        for head in range(N_HEADS):
            bias = jnp.where(valid, -_alibi_slope(head) * dist_f * LOG2E, NEG_INF)
            bias_ref[1, head] = bias
            bias_ref[0, head] = jnp.where(kj >= w, bias, NEG_INF)

    lane = lax.broadcasted_iota(jnp.int32, (w, LANES), 1)
    k_all = jnp.concatenate([kp_ref[0], kc_ref[0]], axis=0)
    v_all = jnp.concatenate([vp_ref[0], vc_ref[0]], axis=0)
    v_lane = lax.broadcasted_iota(jnp.int32, v_all.shape, 1)
    v_swapped = pltpu.roll(v_all, HEAD_DIM, axis=1)
    ones = jnp.ones_like(v_all)
    v_aug = []
    for kv in range(SWA_KV_HEADS):
        v_lo, v_hi = (v_all, v_swapped) if kv == 0 else (v_swapped, v_all)
        v_aug.append(jnp.concatenate([jnp.where(v_lane < HEAD_DIM, v_lo, ones),
                                      jnp.where(v_lane < HEAD_DIM, ones, v_hi)], axis=1))

    n_h = SWA_HEADS_PER_JOB
    jobs = [(i, h0) for i in range(SWA_BLOCKS_PER_STEP) for h0 in range(0, N_HEADS, n_h)]

    def scores(job):
        i, h0 = job
        qx = jnp.concatenate(
            [q_ref[0, i * w:(i + 1) * w, head * LANES:(head + 1) * LANES]
             for head in range(h0, h0 + n_h)], axis=0)
        return lax.dot_general(qx, k_all[i * w:(i + 2) * w], (((1,), (1,)), ((), ())),
                               preferred_element_type=F32)

    def absorb(job, s_all):
        i, h0 = job
        kv = h0 // SWA_GROUP
        table = jnp.minimum(step, 1) if i == 0 else 1
        p_rows, sink_p = [], []
        for g in range(n_h):
            sink = sinks_ref[h0 + g] * LOG2E
            s = s_all[g * w:(g + 1) * w] + bias_ref[table, h0 + g]
            m = jnp.maximum(jnp.max(s, axis=-1, keepdims=True), sink)
            p_rows.append(jnp.exp2(s - m).astype(BF16))
            sink_p.append(jnp.exp2(sink - m))
        acc = jnp.dot(jnp.concatenate(p_rows, axis=0), v_aug[kv][i * w:(i + 2) * w],
                      preferred_element_type=F32)
        for pair in range(n_h // HEADS_PER_LANE_BLOCK):
            num, den = [], []
            for d in range(HEADS_PER_LANE_BLOCK):
                g = pair * HEADS_PER_LANE_BLOCK + d
                a = acc[g * w:(g + 1) * w]
                num.append(a[:, d * LANES:(d + 1) * LANES])
                den.append(a[:, (1 - d) * LANES:(2 - d) * LANES] + sink_p[g])
            blk = h0 // HEADS_PER_LANE_BLOCK + pair
            o_ref[0, i * w:(i + 1) * w, blk * LANES:(blk + 1) * LANES] = (
                jnp.where(lane < HEAD_DIM, num[0], num[1])
                / jnp.where(lane < HEAD_DIM, den[0], den[1])).astype(BF16)

    pending = []
    for job in jobs:
        pending.append((job, scores(job)))
        if len(pending) > SWA_SCORES_AHEAD:
            absorb(*pending.pop(0))
    for item in pending:
        absorb(*item)


def _swa_attn(q, k, v, sinks):
    b, s, _ = q.shape
    w = WINDOW
    n = SWA_BLOCKS_PER_STEP
    kv_w = SWA_KV_HEADS * HEAD_DIM
    prev = pl.BlockSpec((1, w, kv_w), lambda i, j, *_: (i, jnp.maximum(j * n - 1, 0), 0))
    cur = pl.BlockSpec((1, n * w, kv_w), lambda i, j, *_: (i, j, 0))
    grid_spec = pltpu.PrefetchScalarGridSpec(
        num_scalar_prefetch=1,
        grid=(b, s // (n * w)),
        in_specs=[pl.BlockSpec((1, n * w, N_HEADS * LANES), lambda i, j, *_: (i, j, 0)),
                  prev, cur, prev, cur],
        out_specs=pl.BlockSpec((1, n * w, D_MODEL), lambda i, j, *_: (i, j, 0)),
        scratch_shapes=[pltpu.VMEM((2, N_HEADS, w, 2 * w), F32)],
    )
    return pl.pallas_call(
        _swa_attn_kernel,
        grid_spec=grid_spec,
        out_shape=jax.ShapeDtypeStruct((b, s, D_MODEL), BF16),
        compiler_params=_params("arbitrary", "arbitrary"),
        name="swa_attn",
    )(sinks.astype(F32), q, k, k, v, v)


def kernel(x, l0_ffn1_norm, l0_ffn1_w_gate, l0_ffn1_w_up, l0_ffn1_w_down, l0_mix_norm, l0_fox_w_in, l0_fox_b_forget, l0_fox_w_out, l0_ffn2_norm, l0_ffn2_w_gate, l0_ffn2_w_up, l0_ffn2_w_down, l1_ffn1_norm, l1_ffn1_w_gate, l1_ffn1_w_up, l1_ffn1_w_down, l1_mix_norm, l1_swa_w_in, l1_swa_sinks, l1_swa_w_out, l1_ffn2_norm, l1_ffn2_w_gate, l1_ffn2_w_up, l1_ffn2_w_down, final_norm):
    b, s, d = x.shape
    t = b * s
    h = x.reshape(t, d)

    def whole(*weights):
        return [(w, w.shape[1]) for w in weights]


    h, (fox_w_qkv, fox_w_out, wg, wu, wd) = _ffn(
        h, l0_ffn1_norm, l0_ffn1_w_gate, l0_ffn1_w_up, l0_ffn1_w_down,
        casts=[(l0_fox_w_in, 3 * d)] + whole(l0_fox_w_out, l0_ffn2_w_gate, l0_ffn2_w_up,
                                             l0_ffn2_w_down))
    q, k, v = _fox_proj(h.reshape(b, s, d), l0_mix_norm, l0_fox_w_in, fox_w_qkv,
                        l0_fox_b_forget)
    o = _fox_attn(q, k, v)
    h, (wg, wu, wd) = _ffn(h, l0_ffn2_norm, wg, wu, wd, attn=o.reshape(t, d), w_out=fox_w_out,
                           casts=whole(l1_ffn1_w_gate, l1_ffn1_w_up, l1_ffn1_w_down))

    h, (swa_w_in, swa_w_out, wg2, wu2, wd2) = _ffn(
        h, l1_ffn1_norm, wg, wu, wd,
        casts=whole(l1_swa_w_in, l1_swa_w_out, l1_ffn2_w_gate, l1_ffn2_w_up, l1_ffn2_w_down))
    q, k, v = _swa_proj(h, l1_mix_norm, swa_w_in)
    kv_w = SWA_KV_HEADS * HEAD_DIM
    o = _swa_attn(q.reshape(b, s, N_HEADS * LANES), k.reshape(b, s, kv_w),
                  v.reshape(b, s, kv_w), l1_swa_sinks)
    h, _ = _ffn(h, l1_ffn2_norm, wg2, wu2, wd2, attn=o.reshape(t, d), w_out=swa_w_out,
                final_g=final_norm)
    return h.reshape(b, s, d)
```

```python
import functools

import jax
import jax.numpy as jnp
import numpy as np
from jax import lax
from jax.experimental import pallas as pl
from jax.experimental.pallas import tpu as pltpu

F32 = jnp.float32
BF16 = jnp.bfloat16

D_MODEL = 1024
HEAD_DIM = 64
N_HEADS = D_MODEL // HEAD_DIM
SWA_KV_HEADS = 2
SWA_GROUP = N_HEADS // SWA_KV_HEADS
WINDOW = 128
RMS_EPS = 1e-6
NEG_INF = -1e30
QK_SCALE = HEAD_DIM ** -0.5

LANES = 128
BF16_TILE_ROWS = 16
CAST_BLOCK_ROWS = 128
HEADS_PER_LANE_BLOCK = LANES // HEAD_DIM
VMEM_LIMIT_BYTES = 56 * 1024 * 1024

TOKEN_TILE = 512
FFN_TILE = 1024
FFN_CHUNK = 512
FOX_TILE = 512
FOX_HEADS_PER_STEP = 4
FOX_SCORES_AHEAD = 4
SWA_PROJ_TILE = 1024
SWA_BLOCKS_PER_STEP = 16
SWA_HEADS_PER_JOB = 4
SWA_SCORES_AHEAD = 1


def _rmsnorm(x, g):
    ms = jnp.mean(x * x, axis=-1, keepdims=True)
    return x * lax.rsqrt(ms + RMS_EPS) * g


def _resident(shape):
    zeros = (0,) * len(shape)
    return pl.BlockSpec(shape, lambda *_: zeros, pipeline_mode=pl.Buffered(1))


def _params(*semantics):
    return pltpu.CompilerParams(dimension_semantics=semantics,
                                vmem_limit_bytes=VMEM_LIMIT_BYTES)


def _ffn_kernel(*refs, d_ff, mixer_out, final_norm, n_casts):
    refs = list(refs)
    h_ref = refs.pop(0)
    attn_ref, wo_ref = (refs.pop(0), refs.pop(0)) if mixer_out else (None, None)
    g_ref, wg_ref, wu_ref, wd_ref = (refs.pop(0) for _ in range(4))
    gf_ref = refs.pop(0) if final_norm else None
    cast_in = [refs.pop(0) for _ in range(n_casts)]
    o_ref = refs.pop(0)
    cast_out = [refs.pop(0) for _ in range(n_casts)]
    a_ref, = refs
    for src, dst in zip(cast_in, cast_out):
        dst[...] = src[...].astype(BF16)
    x = h_ref[...]
    if mixer_out:
        x = x + jnp.dot(attn_ref[...], wo_ref[...], preferred_element_type=F32)
    xn = _rmsnorm(x, g_ref[...]).astype(BF16)
    for c0 in range(0, d_ff, FFN_CHUNK):
        cw = min(FFN_CHUNK, d_ff - c0)
        gate = jnp.dot(xn, wg_ref[:, c0:c0 + cw], preferred_element_type=F32)
        up = jnp.dot(xn, wu_ref[:, c0:c0 + cw], preferred_element_type=F32)
        a_ref[:, c0:c0 + cw] = (gate * jax.nn.sigmoid(gate) * up).astype(BF16)
    y = jnp.dot(a_ref[...], wd_ref[...], preferred_element_type=F32)
    out = x + 0.5 * y
    if final_norm:
        out = _rmsnorm(out, gf_ref[...])
    o_ref[...] = out


def _cast_spec(rows, cols, n_steps):
    block = rows // n_steps
    if rows % n_steps or block % BF16_TILE_ROWS:
        block = CAST_BLOCK_ROWS
    last = rows // block - 1
    return pl.BlockSpec((block, cols), lambda i: (jnp.minimum(i, last), 0))


def _ffn(h, g, wg, wu, wd, attn=None, w_out=None, final_g=None, casts=()):
    t, d = h.shape
    d_ff = wg.shape[1]
    n_steps = t // FFN_TILE
    tile = pl.BlockSpec((FFN_TILE, d), lambda i: (i, 0))
    in_specs, args = [tile], [h]
    if attn is not None:
        in_specs += [pl.BlockSpec((FFN_TILE, attn.shape[1]), lambda i: (i, 0)),
                     _resident(w_out.shape)]
        args += [attn, w_out.astype(BF16)]
    in_specs += [_resident((1, d)), _resident((d, d_ff)), _resident((d, d_ff)),
                 _resident((d_ff, d))]
    args += [g.reshape(1, d), wg.astype(BF16), wu.astype(BF16), wd.astype(BF16)]
    if final_g is not None:
        in_specs.append(_resident((1, d)))
        args.append(final_g.reshape(1, d))
    cast_specs = [_cast_spec(w.shape[0], cols, n_steps) for w, cols in casts]
    outs = pl.pallas_call(
        functools.partial(_ffn_kernel, d_ff=d_ff, mixer_out=attn is not None,
                          final_norm=final_g is not None, n_casts=len(casts)),
        grid=(n_steps,),
        in_specs=in_specs + cast_specs,
        out_specs=[tile] + cast_specs,
        out_shape=[jax.ShapeDtypeStruct((t, d), F32)]
        + [jax.ShapeDtypeStruct((w.shape[0], cols), BF16) for w, cols in casts],
        scratch_shapes=[pltpu.VMEM((FFN_TILE, d_ff), BF16)],
        compiler_params=_params("arbitrary"),
        name="ffn",
    )(*args, *(w for w, _ in casts))
    return outs[0], outs[1:]


def _split_bf16x3(x):
    hi = x.astype(BF16)
    r = x - hi.astype(F32)
    mid = r.astype(BF16)
    lo = (r - mid.astype(F32)).astype(BF16)
    return hi, mid, lo


AUG_PARTS = 3
LOG2E = float(np.log2(np.e))


def _data_half(head):
    return head % HEADS_PER_LANE_BLOCK


def _aug_base(head):
    return HEAD_DIM * (1 - _data_half(head))


def _fox_aug_tables():
    width = N_HEADS * LANES
    place = np.zeros((LANES, width), np.float32)
    const = np.zeros((1, width), np.float32)
    for head in range(N_HEADS):
        q_base = head * LANES + _aug_base(head)
        k_base = head * LANES + (_aug_base(head) + HEAD_DIM) % LANES
        for part in range(AUG_PARTS):
            place[part * N_HEADS + head, q_base + part] = 1.0
            place[part * N_HEADS + head, k_base + AUG_PARTS + part] = -1.0
            const[0, q_base + AUG_PARTS + part] = 1.0
            const[0, k_base + part] = 1.0
    return jnp.asarray(place, BF16), jnp.asarray(const, F32)


def _fox_proj_kernel(h_ref, g_ref, wqkv_ref, wvt_ref, wf_ref, bf_ref, place_ref, const_ref,
                     q_ref, k_ref, vt_ref, carry_ref):
    @pl.when(pl.program_id(1) == 0)
    def _():
        carry_ref[...] = jnp.zeros_like(carry_ref)

    tm = h_ref.shape[1]
    xn = _rmsnorm(h_ref[0], g_ref[...]).astype(BF16)
    lane = lax.broadcasted_iota(jnp.int32, (tm, LANES), 1)

    def project(which):
        return jnp.dot(xn, wqkv_ref[:, which * D_MODEL:(which + 1) * D_MODEL],
                       preferred_element_type=F32)

    def head_blocks(proj, fill):
        for head in range(N_HEADS):
            src = (head // HEADS_PER_LANE_BLOCK) * LANES
            dst = slice(head * LANES, (head + 1) * LANES)
            yield dst, jnp.where(_lane_half_mask(lane, _data_half(head)),
                                 proj[:, src:src + LANES], fill(head, dst)).astype(BF16)

    def by_part(parts):
        out = jnp.zeros((tm, LANES), F32)
        for p in reversed(range(AUG_PARTS)):
            out = jnp.where(lane < (p + 1) * N_HEADS, parts[p], out)
        return out.astype(BF16)


    f_logit = jnp.dot(xn, wf_ref[...], preferred_element_type=F32) + bf_ref[...]
    v_t = lax.dot_general(wvt_ref[...], xn, (((1,), (1,)), ((), ())),
                          preferred_element_type=F32)
    log_f = jnp.minimum(f_logit, 0.0) - jnp.log1p(jnp.exp(-jnp.abs(f_logit)))
    parts = by_part([p.astype(F32) for p in _split_bf16x3(log_f)])
    t_out = lax.broadcasted_iota(jnp.int32, (tm, tm), 0)
    t_in = lax.broadcasted_iota(jnp.int32, (tm, tm), 1)
    tri = (t_in <= t_out).astype(BF16)
    sums = jnp.dot(tri, parts, preferred_element_type=F32)
    q_proj = project(0) * (QK_SCALE * LOG2E)
    ones_row = (lax.broadcasted_iota(jnp.int32, (LANES - HEAD_DIM, tm), 0) == 0).astype(BF16)
    for head in range(N_HEADS):
        vt_ref[0, head * LANES:head * LANES + HEAD_DIM, :] = (
            v_t[head * HEAD_DIM:(head + 1) * HEAD_DIM].astype(BF16))
        vt_ref[0, head * LANES + HEAD_DIM:(head + 1) * LANES, :] = ones_row

    c = sums + carry_ref[...]
    for p in range(1, AUG_PARTS):
        c = c + pltpu.roll(sums, LANES - p * N_HEADS, axis=1)
    carry_ref[...] = c[tm - 1:tm, :]
    c_parts = [p.astype(F32) for p in _split_bf16x3(c * LOG2E)]
    c_parts = by_part([c_parts[0]] + [pltpu.roll(c_parts[p], p * N_HEADS, axis=1)
                                      for p in range(1, AUG_PARTS)])
    aug = jnp.dot(c_parts, place_ref[...], preferred_element_type=F32) + const_ref[...]
    k_proj = project(1)
    for dst, blk in head_blocks(q_proj, lambda head, dst: aug[:, dst]):
        q_ref[0, :, dst] = blk
    for dst, blk in head_blocks(k_proj, lambda head, dst: pltpu.roll(aug[:, dst], HEAD_DIM,
                                                                     axis=1)):
        k_ref[0, :, dst] = blk


def _fox_proj(h, g, w_in, w_qkv, b_forget):
    b, s, d = h.shape
    pad = LANES - AUG_PARTS * N_HEADS
    w_f = jnp.pad(jnp.tile(w_in[:, 3 * d:], (1, AUG_PARTS)), ((0, 0), (0, pad))).astype(BF16)
    b_f = jnp.pad(jnp.tile(b_forget.astype(F32), AUG_PARTS), (0, pad)).reshape(1, LANES)
    w_vt = w_qkv[:, 2 * d:].T
    place, const = _fox_aug_tables()
    tm = TOKEN_TILE
    width = N_HEADS * LANES
    out_spec = pl.BlockSpec((1, tm, width), lambda i, j: (i, j, 0))
    out_shape = jax.ShapeDtypeStruct((b, s, width), BF16)
    return pl.pallas_call(
        _fox_proj_kernel,
        grid=(b, s // tm),
        in_specs=[pl.BlockSpec((1, tm, d), lambda i, j: (i, j, 0)),
                  _resident((1, d)), _resident(w_qkv.shape), _resident(w_vt.shape),
                  _resident(w_f.shape), _resident(b_f.shape), _resident(place.shape),
                  _resident(const.shape)],
        out_specs=[out_spec, out_spec, pl.BlockSpec((1, width, tm), lambda i, j: (i, 0, j))],
        out_shape=[out_shape, out_shape, jax.ShapeDtypeStruct((b, width, s), BF16)],
        scratch_shapes=[pltpu.VMEM((1, LANES), F32)],
        compiler_params=_params("parallel", "arbitrary"),
        name="fox_proj",
    )(h, g.reshape(1, d), w_qkv, w_vt, w_f, b_f, place, const)


def _fox_attn_kernel(q_ref, k_ref, vt_ref, o_ref):
    seq = q_ref.shape[1]
    t = FOX_TILE
    half = t // 2
    heads = range(FOX_HEADS_PER_STEP)
    first_mask = (lax.broadcasted_iota(jnp.int32, (half, half), 0)
                  <= lax.broadcasted_iota(jnp.int32, (half, half), 1))
    second_mask = (lax.broadcasted_iota(jnp.int32, (t, half), 0)
                   <= lax.broadcasted_iota(jnp.int32, (t, half), 1) + half)

    cols = [slice(hh * LANES, (hh + 1) * LANES) for hh in heads]

    jobs = []
    for qi in range(seq // t):
        q0 = qi * t
        for j in range(qi):
            jobs += [(qi, hh, (q0, t), (j * t, t), None) for hh in heads]
        for hh in heads:
            jobs.append((qi, hh, (q0, half), (q0, half), first_mask))
            jobs.append((qi, hh, (q0 + half, half), (q0, t), second_mask))

    state = {}
    finished = {}

    def scores(job):
        _, hh, (r0, rn), (k0, kn), mask = job
        s = lax.dot_general(k_ref[0, k0:k0 + kn, cols[hh]], q_ref[0, r0:r0 + rn, cols[hh]],
                            (((1,), (1,)), ((), ())), preferred_element_type=F32)
        return s if mask is None else jnp.where(mask, s, NEG_INF)

    def absorb(job, s):
        qi, hh, (r0, rn), (k0, kn), mask = job
        m, acc = state.get((qi, hh), (jnp.full((1, t), NEG_INF, F32),
                                      jnp.zeros((LANES, t), F32)))
        if mask is not None:
            lo = r0 - qi * t
            m, acc = m[:, lo:lo + rn], acc[:, lo:lo + rn]
        m_new = jnp.maximum(m, jnp.max(s, axis=0, keepdims=True))
        p = jnp.exp2(s - m_new).astype(BF16)
        acc = jnp.exp2(m - m_new) * acc + jnp.dot(vt_ref[0, cols[hh], k0:k0 + kn], p,
                                                  preferred_element_type=F32)
        if mask is None:
            state[(qi, hh)] = (m_new, acc)
            return
        pair = hh // HEADS_PER_LANE_BLOCK
        accs = finished.setdefault((r0, pair), [])
        accs.append(acc)
        if len(accs) == HEADS_PER_LANE_BLOCK:
            o_t = jnp.concatenate([a[:HEAD_DIM] / a[HEAD_DIM:HEAD_DIM + 1] for a in accs],
                                  axis=0)
            o_ref[0, r0:r0 + rn, cols[pair]] = o_t.T.astype(BF16)

    pending = []
    for job in jobs:
        pending.append((job, scores(job)))
        if len(pending) > FOX_SCORES_AHEAD:
            absorb(*pending.pop(0))
    for item in pending:
        absorb(*item)


def _fox_attn(q, k, v_t):
    b, s, _ = q.shape
    in_w = FOX_HEADS_PER_STEP * LANES
    blk = pl.BlockSpec((1, s, in_w), lambda i, p: (i, 0, p))
    return pl.pallas_call(
        _fox_attn_kernel,
        grid=(b, N_HEADS // FOX_HEADS_PER_STEP),
        in_specs=[blk, blk, pl.BlockSpec((1, in_w, s), lambda i, p: (i, p, 0))],
        out_specs=pl.BlockSpec((1, s, FOX_HEADS_PER_STEP * HEAD_DIM), lambda i, p: (i, 0, p)),
        out_shape=jax.ShapeDtypeStruct((b, s, D_MODEL), BF16),
        compiler_params=_params("parallel", "parallel"),
        name="fox_attn",
    )(q, k, v_t)


def _lane_half_mask(lane, half):
    return (lane >= half * HEAD_DIM) & (lane < (half + 1) * HEAD_DIM)


def _swa_proj_kernel(h_ref, g_ref, w_ref, q_ref, k_ref, v_ref):
    tm = h_ref.shape[0]
    xn = _rmsnorm(h_ref[...], g_ref[...]).astype(BF16)
    qkv = jnp.dot(xn, w_ref[...], preferred_element_type=F32)
    kv_w = SWA_KV_HEADS * HEAD_DIM
    k_ref[...] = qkv[:, D_MODEL:D_MODEL + kv_w].astype(BF16)
    v_ref[...] = qkv[:, D_MODEL + kv_w:].astype(BF16)
    lane = lax.broadcasted_iota(jnp.int32, (tm, LANES), 1)
    for head in range(N_HEADS):
        kv = head // SWA_GROUP
        src = (head // HEADS_PER_LANE_BLOCK) * LANES
        q = qkv[:, src:src + LANES] * (QK_SCALE * LOG2E)
        if head % HEADS_PER_LANE_BLOCK != kv:
            q = pltpu.roll(q, HEAD_DIM, axis=1)
        q_ref[:, head * LANES:(head + 1) * LANES] = jnp.where(
            _lane_half_mask(lane, kv), q, 0.0).astype(BF16)


def _swa_proj(h, g, w_in):
    t, d = h.shape
    kv_w = SWA_KV_HEADS * HEAD_DIM
    tm = SWA_PROJ_TILE
    return pl.pallas_call(
        _swa_proj_kernel,
        grid=(t // tm,),
        in_specs=[pl.BlockSpec((tm, d), lambda i: (i, 0)), _resident((1, d)),
                  _resident(w_in.shape)],
        out_specs=[pl.BlockSpec((tm, N_HEADS * LANES), lambda i: (i, 0)),
                   pl.BlockSpec((tm, kv_w), lambda i: (i, 0)),
                   pl.BlockSpec((tm, kv_w), lambda i: (i, 0))],
        out_shape=[jax.ShapeDtypeStruct((t, N_HEADS * LANES), BF16),
                   jax.ShapeDtypeStruct((t, kv_w), BF16),
                   jax.ShapeDtypeStruct((t, kv_w), BF16)],
        compiler_params=_params("parallel"),
        name="swa_proj",
    )(h, g.reshape(1, d), w_in.astype(BF16))


def _alibi_slope(head):
    return float(np.float32(2.0 ** (-8.0 * (head + 1) / N_HEADS)))


def _swa_attn_kernel(sinks_ref, q_ref, kp_ref, kc_ref, vp_ref, vc_ref, o_ref, bias_ref):
    w = WINDOW
    step = pl.program_id(1)

    @pl.when((pl.program_id(0) == 0) & (step == 0))
    def _():
        qi = lax.broadcasted_iota(jnp.int32, (w, 2 * w), 0)
        kj = lax.broadcasted_iota(jnp.int32, (w, 2 * w), 1)
        dist = qi + w - kj
        valid = (dist >= 0) & (dist < w)
        dist_f = dist.astype(F32)
        for head in range(N_HEADS):
            bias = jnp.where(valid, -_alibi_slope(head) * dist_f * LOG2E, NEG_INF)
            bias_ref[1, head] = bias
            bias_ref[0, head] = jnp.where(kj >= w, bias, NEG_INF)

    lane = lax.broadcasted_iota(jnp.int32, (w, LANES), 1)
    k_all = jnp.concatenate([kp_ref[0], kc_ref[0]], axis=0)
    v_all = jnp.concatenate([vp_ref[0], vc_ref[0]], axis=0)
    v_lane = lax.broadcasted_iota(jnp.int32, v_all.shape, 1)
    v_swapped = pltpu.roll(v_all, HEAD_DIM, axis=1)
    ones = jnp.ones_like(v_all)
    v_aug = []
    for kv in range(SWA_KV_HEADS):
        v_lo, v_hi = (v_all, v_swapped) if kv == 0 else (v_swapped, v_all)
        v_aug.append(jnp.concatenate([jnp.where(v_lane < HEAD_DIM, v_lo, ones),
                                      jnp.where(v_lane < HEAD_DIM, ones, v_hi)], axis=1))

    n_h = SWA_HEADS_PER_JOB
    jobs = [(i, h0) for i in range(SWA_BLOCKS_PER_STEP) for h0 in range(0, N_HEADS, n_h)]

    def scores(job):
        i, h0 = job
        qx = jnp.concatenate(
            [q_ref[0, i * w:(i + 1) * w, head * LANES:(head + 1) * LANES]
             for head in range(h0, h0 + n_h)], axis=0)
        return lax.dot_general(qx, k_all[i * w:(i + 2) * w], (((1,), (1,)), ((), ())),
                               preferred_element_type=F32)

    def absorb(job, s_all):
        i, h0 = job
        kv = h0 // SWA_GROUP
        table = jnp.minimum(step, 1) if i == 0 else 1
        p_rows, sink_p = [], []
        for g in range(n_h):
            sink = sinks_ref[h0 + g] * LOG2E
            s = s_all[g * w:(g + 1) * w] + bias_ref[table, h0 + g]
            m = jnp.maximum(jnp.max(s, axis=-1, keepdims=True), sink)
            p_rows.append(jnp.exp2(s - m).astype(BF16))
            sink_p.append(jnp.exp2(sink - m))
        acc = jnp.dot(jnp.concatenate(p_rows, axis=0), v_aug[kv][i * w:(i + 2) * w],
                      preferred_element_type=F32)
        for pair in range(n_h // HEADS_PER_LANE_BLOCK):
            num, den = [], []
            for d in range(HEADS_PER_LANE_BLOCK):
                g = pair * HEADS_PER_LANE_BLOCK + d
                a = acc[g * w:(g + 1) * w]
                num.append(a[:, d * LANES:(d + 1) * LANES])
                den.append(a[:, (1 - d) * LANES:(2 - d) * LANES] + sink_p[g])
            blk = h0 // HEADS_PER_LANE_BLOCK + pair
            o_ref[0, i * w:(i + 1) * w, blk * LANES:(blk + 1) * LANES] = (
                jnp.where(lane < HEAD_DIM, num[0], num[1])
                / jnp.where(lane < HEAD_DIM, den[0], den[1])).astype(BF16)

    pending = []
    for job in jobs:
        pending.append((job, scores(job)))
        if len(pending) > SWA_SCORES_AHEAD:
            absorb(*pending.pop(0))
    for item in pending:
        absorb(*item)


def _swa_attn(q, k, v, sinks):
    b, s, _ = q.shape
    w = WINDOW
    n = SWA_BLOCKS_PER_STEP
    kv_w = SWA_KV_HEADS * HEAD_DIM
    prev = pl.BlockSpec((1, w, kv_w), lambda i, j, *_: (i, jnp.maximum(j * n - 1, 0), 0))
    cur = pl.BlockSpec((1, n * w, kv_w), lambda i, j, *_: (i, j, 0))
    grid_spec = pltpu.PrefetchScalarGridSpec(
        num_scalar_prefetch=1,
        grid=(b, s // (n * w)),
        in_specs=[pl.BlockSpec((1, n * w, N_HEADS * LANES), lambda i, j, *_: (i, j, 0)),
                  prev, cur, prev, cur],
        out_specs=pl.BlockSpec((1, n * w, D_MODEL), lambda i, j, *_: (i, j, 0)),
        scratch_shapes=[pltpu.VMEM((2, N_HEADS, w, 2 * w), F32)],
    )
    return pl.pallas_call(
        _swa_attn_kernel,
        grid_spec=grid_spec,
        out_shape=jax.ShapeDtypeStruct((b, s, D_MODEL), BF16),
        compiler_params=_params("arbitrary", "arbitrary"),
        name="swa_attn",
    )(sinks.astype(F32), q, k, k, v, v)


def kernel(x, l0_ffn1_norm, l0_ffn1_w_gate, l0_ffn1_w_up, l0_ffn1_w_down, l0_mix_norm, l0_fox_w_in, l0_fox_b_forget, l0_fox_w_out, l0_ffn2_norm, l0_ffn2_w_gate, l0_ffn2_w_up, l0_ffn2_w_down, l1_ffn1_norm, l1_ffn1_w_gate, l1_ffn1_w_up, l1_ffn1_w_down, l1_mix_norm, l1_swa_w_in, l1_swa_sinks, l1_swa_w_out, l1_ffn2_norm, l1_ffn2_w_gate, l1_ffn2_w_up, l1_ffn2_w_down, final_norm):
    b, s, d = x.shape
    t = b * s
    h = x.reshape(t, d)

    def whole(*weights):
        return [(w, w.shape[1]) for w in weights]


    h, (fox_w_qkv, fox_w_out, wg, wu, wd) = _ffn(
        h, l0_ffn1_norm, l0_ffn1_w_gate, l0_ffn1_w_up, l0_ffn1_w_down,
        casts=[(l0_fox_w_in, 3 * d)] + whole(l0_fox_w_out, l0_ffn2_w_gate, l0_ffn2_w_up,
                                             l0_ffn2_w_down))
    q, k, v = _fox_proj(h.reshape(b, s, d), l0_mix_norm, l0_fox_w_in, fox_w_qkv,
                        l0_fox_b_forget)
    o = _fox_attn(q, k, v)
    h, (wg, wu, wd) = _ffn(h, l0_ffn2_norm, wg, wu, wd, attn=o.reshape(t, d), w_out=fox_w_out,
                           casts=whole(l1_ffn1_w_gate, l1_ffn1_w_up, l1_ffn1_w_down))

    h, (swa_w_in, swa_w_out, wg2, wu2, wd2) = _ffn(
        h, l1_ffn1_norm, wg, wu, wd,
        casts=whole(l1_swa_w_in, l1_swa_w_out, l1_ffn2_w_gate, l1_ffn2_w_up, l1_ffn2_w_down))
    q, k, v = _swa_proj(h, l1_mix_norm, swa_w_in)
    kv_w = SWA_KV_HEADS * HEAD_DIM
    o = _swa_attn(q.reshape(b, s, N_HEADS * LANES), k.reshape(b, s, kv_w),
                  v.reshape(b, s, kv_w), l1_swa_sinks)
    h, _ = _ffn(h, l1_ffn2_norm, wg2, wu2, wd2, attn=o.reshape(t, d), w_out=swa_w_out,
                final_g=final_norm)
    return h.reshape(b, s, d)
```

```python
import functools

import jax
import jax.numpy as jnp
import numpy as np
from jax import lax
from jax.experimental import pallas as pl
from jax.experimental.pallas import tpu as pltpu

F32 = jnp.float32
BF16 = jnp.bfloat16

D_MODEL = 1024
HEAD_DIM = 64
N_HEADS = D_MODEL // HEAD_DIM
SWA_KV_HEADS = 2
SWA_GROUP = N_HEADS // SWA_KV_HEADS
WINDOW = 128
RMS_EPS = 1e-6
NEG_INF = -1e30
QK_SCALE = HEAD_DIM ** -0.5
LOG2E = float(np.log2(np.e))

LANES = 128
BF16_TILE_ROWS = 16
CAST_BLOCK_ROWS = 128
HEADS_PER_LANE_BLOCK = LANES // HEAD_DIM
VMEM_LIMIT_BYTES = 56 * 1024 * 1024

TOKEN_TILE = 512
FFN_TILE = 1024
FFN_CHUNK = 512
FOX_TILE = 512
FOX_HEADS_PER_STEP = 4
FOX_SCORES_AHEAD = 4
SWA_PROJ_TILE = 1024
SWA_BLOCKS_PER_STEP = 16
SWA_HEADS_PER_JOB = 4
SWA_SCORES_AHEAD = 1


def _rmsnorm(x, g):
    ms = jnp.mean(x * x, axis=-1, keepdims=True)
    return x * lax.rsqrt(ms + RMS_EPS) * g


def _lane_half_mask(lane, half):
    return (lane >= half * HEAD_DIM) & (lane < (half + 1) * HEAD_DIM)


def _resident(shape):
    zeros = (0,) * len(shape)
    return pl.BlockSpec(shape, lambda *_: zeros, pipeline_mode=pl.Buffered(1))


def _params(*semantics):
    return pltpu.CompilerParams(dimension_semantics=semantics,
                                vmem_limit_bytes=VMEM_LIMIT_BYTES)


def _ffn_kernel(*refs, d_ff, mixer_out, final_norm, n_casts):
    refs = list(refs)
    h_ref = refs.pop(0)
    attn_ref, wo_ref = (refs.pop(0), refs.pop(0)) if mixer_out else (None, None)
    g_ref, wg_ref, wu_ref, wd_ref = (refs.pop(0) for _ in range(4))
    gf_ref = refs.pop(0) if final_norm else None
    cast_in = [refs.pop(0) for _ in range(n_casts)]
    o_ref = refs.pop(0)
    cast_out = [refs.pop(0) for _ in range(n_casts)]
    a_ref, = refs
    for src, dst in zip(cast_in, cast_out):
        dst[...] = src[...].astype(BF16)
    x = h_ref[...]
    if mixer_out:
        x = x + jnp.dot(attn_ref[...], wo_ref[...], preferred_element_type=F32)
    xn = _rmsnorm(x, g_ref[...]).astype(BF16)
    for c0 in range(0, d_ff, FFN_CHUNK):
        cw = min(FFN_CHUNK, d_ff - c0)
        gate = jnp.dot(xn, wg_ref[:, c0:c0 + cw], preferred_element_type=F32)
        up = jnp.dot(xn, wu_ref[:, c0:c0 + cw], preferred_element_type=F32)
        a_ref[:, c0:c0 + cw] = (gate * jax.nn.sigmoid(gate) * up).astype(BF16)
    y = jnp.dot(a_ref[...], wd_ref[...], preferred_element_type=F32)
    out = x + 0.5 * y
    if final_norm:
        out = _rmsnorm(out, gf_ref[...])
    o_ref[...] = out


def _cast_spec(rows, cols, n_steps):
    block = rows // n_steps
    if rows % n_steps or block % BF16_TILE_ROWS:
        block = CAST_BLOCK_ROWS
    last = rows // block - 1
    return pl.BlockSpec((block, cols), lambda i: (jnp.minimum(i, last), 0))


def _ffn(h, g, wg, wu, wd, attn=None, w_out=None, final_g=None, casts=()):
    t, d = h.shape
    d_ff = wg.shape[1]
    n_steps = t // FFN_TILE
    tile = pl.BlockSpec((FFN_TILE, d), lambda i: (i, 0))
    in_specs, args = [tile], [h]
    if attn is not None:
        in_specs += [pl.BlockSpec((FFN_TILE, attn.shape[1]), lambda i: (i, 0)),
                     _resident(w_out.shape)]
        args += [attn, w_out.astype(BF16)]
    in_specs += [_resident((1, d)), _resident((d, d_ff)), _resident((d, d_ff)),
                 _resident((d_ff, d))]
    args += [g.reshape(1, d), wg.astype(BF16), wu.astype(BF16), wd.astype(BF16)]
    if final_g is not None:
        in_specs.append(_resident((1, d)))
        args.append(final_g.reshape(1, d))
    cast_specs = [_cast_spec(w.shape[0], cols, n_steps) for w, cols in casts]
    outs = pl.pallas_call(
        functools.partial(_ffn_kernel, d_ff=d_ff, mixer_out=attn is not None,
                          final_norm=final_g is not None, n_casts=len(casts)),
        grid=(n_steps,),
        in_specs=in_specs + cast_specs,
        out_specs=[tile] + cast_specs,
        out_shape=[jax.ShapeDtypeStruct((t, d), F32)]
        + [jax.ShapeDtypeStruct((w.shape[0], cols), BF16) for w, cols in casts],
        scratch_shapes=[pltpu.VMEM((FFN_TILE, d_ff), BF16)],
        compiler_params=_params("arbitrary"),
        name="ffn",
    )(*args, *(w for w, _ in casts))
    return outs[0], outs[1:]


def _split_bf16x3(x):
    hi = x.astype(BF16)
    r = x - hi.astype(F32)
    mid = r.astype(BF16)
    lo = (r - mid.astype(F32)).astype(BF16)
    return hi, mid, lo


AUG_PARTS = 3


def _data_half(head):
    return head % HEADS_PER_LANE_BLOCK


def _aug_base(head):
    return HEAD_DIM * (1 - _data_half(head))


def _fox_aug_tables():
    width = N_HEADS * LANES
    place = np.zeros((LANES, width), np.float32)
    const = np.zeros((1, width), np.float32)
    for head in range(N_HEADS):
        q_base = head * LANES + _aug_base(head)
        k_base = head * LANES + (_aug_base(head) + HEAD_DIM) % LANES
        for part in range(AUG_PARTS):
            place[part * N_HEADS + head, q_base + part] = 1.0
            place[part * N_HEADS + head, k_base + AUG_PARTS + part] = -1.0
            const[0, q_base + AUG_PARTS + part] = 1.0
            const[0, k_base + part] = 1.0
    return jnp.asarray(place, BF16), jnp.asarray(const, F32)


def _fox_proj_kernel(h_ref, g_ref, wqkv_ref, wvt_ref, wf_ref, bf_ref, place_ref, const_ref,
                     q_ref, k_ref, vt_ref, carry_ref):
    @pl.when(pl.program_id(1) == 0)
    def _():
        carry_ref[...] = jnp.zeros_like(carry_ref)

    tm = h_ref.shape[1]
    xn = _rmsnorm(h_ref[0], g_ref[...]).astype(BF16)
    lane = lax.broadcasted_iota(jnp.int32, (tm, LANES), 1)

    def project(which):
        return jnp.dot(xn, wqkv_ref[:, which * D_MODEL:(which + 1) * D_MODEL],
                       preferred_element_type=F32)

    def head_blocks(proj, fill):
        for head in range(N_HEADS):
            src = (head // HEADS_PER_LANE_BLOCK) * LANES
            dst = slice(head * LANES, (head + 1) * LANES)
            yield dst, jnp.where(_lane_half_mask(lane, _data_half(head)),
                                 proj[:, src:src + LANES], fill(head, dst)).astype(BF16)

    def by_part(parts):
        out = jnp.zeros((tm, LANES), F32)
        for p in reversed(range(AUG_PARTS)):
            out = jnp.where(lane < (p + 1) * N_HEADS, parts[p], out)
        return out.astype(BF16)


    f_logit = jnp.dot(xn, wf_ref[...], preferred_element_type=F32) + bf_ref[...]
    v_t = lax.dot_general(wvt_ref[...], xn, (((1,), (1,)), ((), ())),
                          preferred_element_type=F32)
    log_f = jnp.minimum(f_logit, 0.0) - jnp.log1p(jnp.exp(-jnp.abs(f_logit)))
    parts = by_part([p.astype(F32) for p in _split_bf16x3(log_f)])
    t_out = lax.broadcasted_iota(jnp.int32, (tm, tm), 0)
    t_in = lax.broadcasted_iota(jnp.int32, (tm, tm), 1)
    tri = (t_in <= t_out).astype(BF16)
    sums = jnp.dot(tri, parts, preferred_element_type=F32)
    q_proj = project(0) * (QK_SCALE * LOG2E)
    ones_row = (lax.broadcasted_iota(jnp.int32, (LANES - HEAD_DIM, tm), 0) == 0).astype(BF16)
    for head in range(N_HEADS):
        vt_ref[0, head * LANES:head * LANES + HEAD_DIM, :] = (
            v_t[head * HEAD_DIM:(head + 1) * HEAD_DIM].astype(BF16))
        vt_ref[0, head * LANES + HEAD_DIM:(head + 1) * LANES, :] = ones_row

    c = sums + carry_ref[...]
    for p in range(1, AUG_PARTS):
        c = c + pltpu.roll(sums, LANES - p * N_HEADS, axis=1)
    carry_ref[...] = c[tm - 1:tm, :]
    c_parts = [p.astype(F32) for p in _split_bf16x3(c * LOG2E)]
    c_parts = by_part([c_parts[0]] + [pltpu.roll(c_parts[p], p * N_HEADS, axis=1)
                                      for p in range(1, AUG_PARTS)])
    aug = jnp.dot(c_parts, place_ref[...], preferred_element_type=F32) + const_ref[...]
    k_proj = project(1)
    for dst, blk in head_blocks(q_proj, lambda head, dst: aug[:, dst]):
        q_ref[0, :, dst] = blk
    for dst, blk in head_blocks(k_proj, lambda head, dst: pltpu.roll(aug[:, dst], HEAD_DIM,
                                                                     axis=1)):
        k_ref[0, :, dst] = blk


def _fox_proj(h, g, w_in, w_qkv, b_forget):
    b, s, d = h.shape
    pad = LANES - AUG_PARTS * N_HEADS
    w_f = jnp.pad(jnp.tile(w_in[:, 3 * d:], (1, AUG_PARTS)), ((0, 0), (0, pad))).astype(BF16)
    b_f = jnp.pad(jnp.tile(b_forget.astype(F32), AUG_PARTS), (0, pad)).reshape(1, LANES)
    w_vt = w_qkv[:, 2 * d:].T
    place, const = _fox_aug_tables()
    tm = TOKEN_TILE
    width = N_HEADS * LANES
    out_spec = pl.BlockSpec((1, tm, width), lambda i, j: (i, j, 0))
    out_shape = jax.ShapeDtypeStruct((b, s, width), BF16)
    return pl.pallas_call(
        _fox_proj_kernel,
        grid=(b, s // tm),
        in_specs=[pl.BlockSpec((1, tm, d), lambda i, j: (i, j, 0)),
                  _resident((1, d)), _resident(w_qkv.shape), _resident(w_vt.shape),
                  _resident(w_f.shape), _resident(b_f.shape), _resident(place.shape),
                  _resident(const.shape)],
        out_specs=[out_spec, out_spec, pl.BlockSpec((1, width, tm), lambda i, j: (i, 0, j))],
        out_shape=[out_shape, out_shape, jax.ShapeDtypeStruct((b, width, s), BF16)],
        scratch_shapes=[pltpu.VMEM((1, LANES), F32)],
        compiler_params=_params("parallel", "arbitrary"),
        name="fox_proj",
    )(h, g.reshape(1, d), w_qkv, w_vt, w_f, b_f, place, const)


def _fox_attn_kernel(q_ref, k_ref, vt_ref, o_ref):
    seq = q_ref.shape[1]
    t = FOX_TILE
    half = t // 2
    heads = range(FOX_HEADS_PER_STEP)
    first_mask = (lax.broadcasted_iota(jnp.int32, (half, half), 0)
                  <= lax.broadcasted_iota(jnp.int32, (half, half), 1))
    second_mask = (lax.broadcasted_iota(jnp.int32, (t, half), 0)
                   <= lax.broadcasted_iota(jnp.int32, (t, half), 1) + half)

    cols = [slice(hh * LANES, (hh + 1) * LANES) for hh in heads]

    jobs = []
    for qi in range(seq // t):
        q0 = qi * t
        for j in range(qi):
            jobs += [(qi, hh, (q0, t), (j * t, t), None) for hh in heads]
        for hh in heads:
            jobs.append((qi, hh, (q0, half), (q0, half), first_mask))
            jobs.append((qi, hh, (q0 + half, half), (q0, t), second_mask))

    state = {}
    finished = {}

    def scores(job):
        _, hh, (r0, rn), (k0, kn), mask = job
        s = lax.dot_general(k_ref[0, k0:k0 + kn, cols[hh]], q_ref[0, r0:r0 + rn, cols[hh]],
                            (((1,), (1,)), ((), ())), preferred_element_type=F32)
        return s if mask is None else jnp.where(mask, s, NEG_INF)

    def absorb(job, s):
        qi, hh, (r0, rn), (k0, kn), mask = job
        m, acc = state.get((qi, hh), (jnp.full((1, t), NEG_INF, F32),
                                      jnp.zeros((LANES, t), F32)))
        if mask is not None:
            lo = r0 - qi * t
            m, acc = m[:, lo:lo + rn], acc[:, lo:lo + rn]
        m_new = jnp.maximum(m, jnp.max(s, axis=0, keepdims=True))
        p = jnp.exp2(s - m_new).astype(BF16)
        acc = jnp.exp2(m - m_new) * acc + jnp.dot(vt_ref[0, cols[hh], k0:k0 + kn], p,
                                                  preferred_element_type=F32)
        if mask is None:
            state[(qi, hh)] = (m_new, acc)
            return
        pair = hh // HEADS_PER_LANE_BLOCK
        accs = finished.setdefault((r0, pair), [])
        accs.append(acc)
        if len(accs) == HEADS_PER_LANE_BLOCK:
            o_t = jnp.concatenate([a[:HEAD_DIM] / a[HEAD_DIM:HEAD_DIM + 1] for a in accs],
                                  axis=0)
            o_ref[0, r0:r0 + rn, cols[pair]] = o_t.T.astype(BF16)

    pending = []
    for job in jobs:
        pending.append((job, scores(job)))
        if len(pending) > FOX_SCORES_AHEAD:
            absorb(*pending.pop(0))
    for item in pending:
        absorb(*item)


def _fox_attn(q, k, v_t):
    b, s, _ = q.shape
    in_w = FOX_HEADS_PER_STEP * LANES
    blk = pl.BlockSpec((1, s, in_w), lambda i, p: (i, 0, p))
    return pl.pallas_call(
        _fox_attn_kernel,
        grid=(b, N_HEADS // FOX_HEADS_PER_STEP),
        in_specs=[blk, blk, pl.BlockSpec((1, in_w, s), lambda i, p: (i, p, 0))],
        out_specs=pl.BlockSpec((1, s, FOX_HEADS_PER_STEP * HEAD_DIM), lambda i, p: (i, 0, p)),
        out_shape=jax.ShapeDtypeStruct((b, s, D_MODEL), BF16),
        compiler_params=_params("parallel", "parallel"),
        name="fox_attn",
    )(q, k, v_t)


def _swa_proj_kernel(h_ref, g_ref, w_ref, q_ref, k_ref, v_ref):
    tm = h_ref.shape[0]
    xn = _rmsnorm(h_ref[...], g_ref[...]).astype(BF16)
    qkv = jnp.dot(xn, w_ref[...], preferred_element_type=F32)
    kv_w = SWA_KV_HEADS * HEAD_DIM
    k_ref[...] = qkv[:, D_MODEL:D_MODEL + kv_w].astype(BF16)
    v_ref[...] = qkv[:, D_MODEL + kv_w:].astype(BF16)
    lane = lax.broadcasted_iota(jnp.int32, (tm, LANES), 1)
    for head in range(N_HEADS):
        kv = head // SWA_GROUP
        src = (head // HEADS_PER_LANE_BLOCK) * LANES
        q = qkv[:, src:src + LANES] * (QK_SCALE * LOG2E)
        if head % HEADS_PER_LANE_BLOCK != kv:
            q = pltpu.roll(q, HEAD_DIM, axis=1)
        q_ref[:, head * LANES:(head + 1) * LANES] = jnp.where(
            _lane_half_mask(lane, kv), q, 0.0).astype(BF16)


def _swa_proj(h, g, w_in):
    t, d = h.shape
    kv_w = SWA_KV_HEADS * HEAD_DIM
    tm = SWA_PROJ_TILE
    return pl.pallas_call(
        _swa_proj_kernel,
        grid=(t // tm,),
        in_specs=[pl.BlockSpec((tm, d), lambda i: (i, 0)), _resident((1, d)),
                  _resident(w_in.shape)],
        out_specs=[pl.BlockSpec((tm, N_HEADS * LANES), lambda i: (i, 0)),
                   pl.BlockSpec((tm, kv_w), lambda i: (i, 0)),
                   pl.BlockSpec((tm, kv_w), lambda i: (i, 0))],
        out_shape=[jax.ShapeDtypeStruct((t, N_HEADS * LANES), BF16),
                   jax.ShapeDtypeStruct((t, kv_w), BF16),
                   jax.ShapeDtypeStruct((t, kv_w), BF16)],
        compiler_params=_params("parallel"),
        name="swa_proj",
    )(h, g.reshape(1, d), w_in.astype(BF16))


def _alibi_slope(head):
    return float(np.float32(2.0 ** (-8.0 * (head + 1) / N_HEADS)))


def _swa_attn_kernel(sinks_ref, q_ref, kp_ref, kc_ref, vp_ref, vc_ref, o_ref, bias_ref):
    w = WINDOW
    step = pl.program_id(1)

    @pl.when((pl.program_id(0) == 0) & (step == 0))
    def _():
        qi = lax.broadcasted_iota(jnp.int32, (w, 2 * w), 0)
        kj = lax.broadcasted_iota(jnp.int32, (w, 2 * w), 1)
        dist = qi + w - kj
        valid = (dist >= 0) & (dist < w)
        dist_f = dist.astype(F32)
        for head in range(N_HEADS):
            bias = jnp.where(valid, -_alibi_slope(head) * dist_f * LOG2E, NEG_INF)
            bias_ref[1, head] = bias
            bias_ref[0, head] = jnp.where(kj >= w, bias, NEG_INF)

    lane = lax.broadcasted_iota(jnp.int32, (w, LANES), 1)
    k_all = jnp.concatenate([kp_ref[0], kc_ref[0]], axis=0)
    v_all = jnp.concatenate([vp_ref[0], vc_ref[0]], axis=0)
    v_lane = lax.broadcasted_iota(jnp.int32, v_all.shape, 1)
    v_swapped = pltpu.roll(v_all, HEAD_DIM, axis=1)
    ones = jnp.ones_like(v_all)
    v_aug = []
    for kv in range(SWA_KV_HEADS):
        v_lo, v_hi = (v_all, v_swapped) if kv == 0 else (v_swapped, v_all)
        v_aug.append(jnp.concatenate([jnp.where(v_lane < HEAD_DIM, v_lo, ones),
                                      jnp.where(v_lane < HEAD_DIM, ones, v_hi)], axis=1))

    n_h = SWA_HEADS_PER_JOB
    jobs = [(i, h0) for i in range(SWA_BLOCKS_PER_STEP) for h0 in range(0, N_HEADS, n_h)]

    def scores(job):
        i, h0 = job
        qx = jnp.concatenate(
            [q_ref[0, i * w:(i + 1) * w, head * LANES:(head + 1) * LANES]
             for head in range(h0, h0 + n_h)], axis=0)
        return lax.dot_general(qx, k_all[i * w:(i + 2) * w], (((1,), (1,)), ((), ())),
                               preferred_element_type=F32)

    def absorb(job, s_all):
        i, h0 = job
        kv = h0 // SWA_GROUP
        table = jnp.minimum(step, 1) if i == 0 else 1
        p_rows, sink_p = [], []
        for g in range(n_h):
            sink = sinks_ref[h0 + g] * LOG2E
            s = s_all[g * w:(g + 1) * w] + bias_ref[table, h0 + g]
            m = jnp.maximum(jnp.max(s, axis=-1, keepdims=True), sink)
            p_rows.append(jnp.exp2(s - m).astype(BF16))
            sink_p.append(jnp.exp2(sink - m))
        acc = jnp.dot(jnp.concatenate(p_rows, axis=0), v_aug[kv][i * w:(i + 2) * w],
                      preferred_element_type=F32)
        for pair in range(n_h // HEADS_PER_LANE_BLOCK):
            num, den = [], []
            for d in range(HEADS_PER_LANE_BLOCK):
                g = pair * HEADS_PER_LANE_BLOCK + d
                a = acc[g * w:(g + 1) * w]
                num.append(a[:, d * LANES:(d + 1) * LANES])
                den.append(a[:, (1 - d) * LANES:(2 - d) * LANES] + sink_p[g])
            blk = h0 // HEADS_PER_LANE_BLOCK + pair
            o_ref[0, i * w:(i + 1) * w, blk * LANES:(blk + 1) * LANES] = (
                jnp.where(lane < HEAD_DIM, num[0], num[1])
                / jnp.where(lane < HEAD_DIM, den[0], den[1])).astype(BF16)

    pending = []
    for job in jobs:
        pending.append((job, scores(job)))
        if len(pending) > SWA_SCORES_AHEAD:
            absorb(*pending.pop(0))
    for item in pending:
        absorb(*item)


def _swa_attn(q, k, v, sinks):
    b, s, _ = q.shape
    w = WINDOW
    n = SWA_BLOCKS_PER_STEP
    kv_w = SWA_KV_HEADS * HEAD_DIM
    prev = pl.BlockSpec((1, w, kv_w), lambda i, j, *_: (i, jnp.maximum(j * n - 1, 0), 0))
    cur = pl.BlockSpec((1, n * w, kv_w), lambda i, j, *_: (i, j, 0))
    grid_spec = pltpu.PrefetchScalarGridSpec(
        num_scalar_prefetch=1,
        grid=(b, s // (n * w)),
        in_specs=[pl.BlockSpec((1, n * w, N_HEADS * LANES), lambda i, j, *_: (i, j, 0)),
                  prev, cur, prev, cur],
        out_specs=pl.BlockSpec((1, n * w, D_MODEL), lambda i, j, *_: (i, j, 0)),
        scratch_shapes=[pltpu.VMEM((2, N_HEADS, w, 2 * w), F32)],
    )
    return pl.pallas_call(
        _swa_attn_kernel,
        grid_spec=grid_spec,
        out_shape=jax.ShapeDtypeStruct((b, s, D_MODEL), BF16),
        compiler_params=_params("arbitrary", "arbitrary"),
        name="swa_attn",
    )(sinks.astype(F32), q, k, k, v, v)


def kernel(x, l0_ffn1_norm, l0_ffn1_w_gate, l0_ffn1_w_up, l0_ffn1_w_down, l0_mix_norm, l0_fox_w_in, l0_fox_b_forget, l0_fox_w_out, l0_ffn2_norm, l0_ffn2_w_gate, l0_ffn2_w_up, l0_ffn2_w_down, l1_ffn1_norm, l1_ffn1_w_gate, l1_ffn1_w_up, l1_ffn1_w_down, l1_mix_norm, l1_swa_w_in, l1_swa_sinks, l1_swa_w_out, l1_ffn2_norm, l1_ffn2_w_gate, l1_ffn2_w_up, l1_ffn2_w_down, final_norm):
    b, s, d = x.shape
    t = b * s
    h = x.reshape(t, d)

    def whole(*weights):
        return [(w, w.shape[1]) for w in weights]


    h, (fox_w_qkv, fox_w_out, wg, wu, wd) = _ffn(
        h, l0_ffn1_norm, l0_ffn1_w_gate, l0_ffn1_w_up, l0_ffn1_w_down,
        casts=[(l0_fox_w_in, 3 * d)] + whole(l0_fox_w_out, l0_ffn2_w_gate, l0_ffn2_w_up,
                                             l0_ffn2_w_down))
    q, k, v_t = _fox_proj(h.reshape(b, s, d), l0_mix_norm, l0_fox_w_in, fox_w_qkv,
                          l0_fox_b_forget)
    o = _fox_attn(q, k, v_t)
    h, (wg, wu, wd) = _ffn(h, l0_ffn2_norm, wg, wu, wd, attn=o.reshape(t, d), w_out=fox_w_out,
                           casts=whole(l1_ffn1_w_gate, l1_ffn1_w_up, l1_ffn1_w_down))

    h, (swa_w_in, swa_w_out, wg2, wu2, wd2) = _ffn(
        h, l1_ffn1_norm, wg, wu, wd,
        casts=whole(l1_swa_w_in, l1_swa_w_out, l1_ffn2_w_gate, l1_ffn2_w_up, l1_ffn2_w_down))
    q, k, v = _swa_proj(h, l1_mix_norm, swa_w_in)
    kv_w = SWA_KV_HEADS * HEAD_DIM
    o = _swa_attn(q.reshape(b, s, N_HEADS * LANES), k.reshape(b, s, kv_w),
                  v.reshape(b, s, kv_w), l1_swa_sinks)
    h, _ = _ffn(h, l1_ffn2_norm, wg2, wu2, wd2, attn=o.reshape(t, d), w_out=swa_w_out,
                final_g=final_norm)
    return h.reshape(b, s, d)
```

```python
import functools

import jax
import jax.numpy as jnp
import numpy as np
from jax import lax
from jax.experimental import pallas as pl
from jax.experimental.pallas import tpu as pltpu

F32 = jnp.float32
BF16 = jnp.bfloat16

D_MODEL = 1024
HEAD_DIM = 64
N_HEADS = D_MODEL // HEAD_DIM
SWA_KV_HEADS = 2
SWA_GROUP = N_HEADS // SWA_KV_HEADS
WINDOW = 128
RMS_EPS = 1e-6
NEG_INF = -1e30
QK_SCALE = HEAD_DIM ** -0.5
LOG2E = float(np.log2(np.e))

LANES = 128
BF16_TILE_ROWS = 16
CAST_BLOCK_ROWS = 128
HEADS_PER_LANE_BLOCK = LANES // HEAD_DIM
VMEM_LIMIT_BYTES = 56 * 1024 * 1024

TOKEN_TILE = 512
FFN_TILE = 1024
FFN_CHUNK = 512
FOX_TILE = 512
FOX_HEADS_PER_STEP = 4
FOX_SCORES_AHEAD = 4
SWA_PROJ_TILE = 1024
SWA_BLOCKS_PER_STEP = 16
SWA_HEADS_PER_JOB = 4
SWA_SCORES_AHEAD = 1


def _rmsnorm(x, g):
    ms = jnp.mean(x * x, axis=-1, keepdims=True)
    return x * lax.rsqrt(ms + RMS_EPS) * g


def _lane_half_mask(lane, half):
    return (lane >= half * HEAD_DIM) & (lane < (half + 1) * HEAD_DIM)


def _resident(shape):
    zeros = (0,) * len(shape)
    return pl.BlockSpec(shape, lambda *_: zeros, pipeline_mode=pl.Buffered(1))


def _params(*semantics):
    return pltpu.CompilerParams(dimension_semantics=semantics,
                                vmem_limit_bytes=VMEM_LIMIT_BYTES)


def _ffn_kernel(*refs, d_ff, mixer_out, final_norm, n_casts):
    refs = list(refs)
    h_ref = refs.pop(0)
    attn_ref, wo_ref = (refs.pop(0), refs.pop(0)) if mixer_out else (None, None)
    g_ref, wg_ref, wu_ref, wd_ref = (refs.pop(0) for _ in range(4))
    gf_ref = refs.pop(0) if final_norm else None
    cast_in = [refs.pop(0) for _ in range(n_casts)]
    o_ref = refs.pop(0)
    cast_out = [refs.pop(0) for _ in range(n_casts)]
    a_ref, = refs
    for src, dst in zip(cast_in, cast_out):
        dst[...] = src[...].astype(BF16)
    x = h_ref[...]
    if mixer_out:
        x = x + jnp.dot(attn_ref[...], wo_ref[...], preferred_element_type=F32)
    xn = _rmsnorm(x, g_ref[...]).astype(BF16)
    for c0 in range(0, d_ff, FFN_CHUNK):
        cw = min(FFN_CHUNK, d_ff - c0)
        gate = jnp.dot(xn, wg_ref[:, c0:c0 + cw], preferred_element_type=F32)
        up = jnp.dot(xn, wu_ref[:, c0:c0 + cw], preferred_element_type=F32)
        a_ref[:, c0:c0 + cw] = (gate * jax.nn.sigmoid(gate) * up).astype(BF16)
    y = jnp.dot(a_ref[...], wd_ref[...], preferred_element_type=F32)
    out = x + 0.5 * y
    if final_norm:
        out = _rmsnorm(out, gf_ref[...])
    o_ref[...] = out


def _cast_spec(rows, cols, n_steps):
    block = rows // n_steps
    if rows % n_steps or block % BF16_TILE_ROWS:
        block = CAST_BLOCK_ROWS
    last = rows // block - 1
    return pl.BlockSpec((block, cols), lambda i: (jnp.minimum(i, last), 0))


def _ffn(h, g, wg, wu, wd, attn=None, w_out=None, final_g=None, casts=()):
    t, d = h.shape
    d_ff = wg.shape[1]
    n_steps = t // FFN_TILE
    tile = pl.BlockSpec((FFN_TILE, d), lambda i: (i, 0))
    in_specs, args = [tile], [h]
    if attn is not None:
        in_specs += [pl.BlockSpec((FFN_TILE, attn.shape[1]), lambda i: (i, 0)),
                     _resident(w_out.shape)]
        args += [attn, w_out.astype(BF16)]
    in_specs += [_resident((1, d)), _resident((d, d_ff)), _resident((d, d_ff)),
                 _resident((d_ff, d))]
    args += [g.reshape(1, d), wg.astype(BF16), wu.astype(BF16), wd.astype(BF16)]
    if final_g is not None:
        in_specs.append(_resident((1, d)))
        args.append(final_g.reshape(1, d))
    cast_specs = [_cast_spec(w.shape[0], cols, n_steps) for w, cols in casts]
    outs = pl.pallas_call(
        functools.partial(_ffn_kernel, d_ff=d_ff, mixer_out=attn is not None,
                          final_norm=final_g is not None, n_casts=len(casts)),
        grid=(n_steps,),
        in_specs=in_specs + cast_specs,
        out_specs=[tile] + cast_specs,
        out_shape=[jax.ShapeDtypeStruct((t, d), F32)]
        + [jax.ShapeDtypeStruct((w.shape[0], cols), BF16) for w, cols in casts],
        scratch_shapes=[pltpu.VMEM((FFN_TILE, d_ff), BF16)],
        compiler_params=_params("arbitrary"),
        name="ffn",
    )(*args, *(w for w, _ in casts))
    return outs[0], outs[1:]


def _split_bf16x3(x):
    hi = x.astype(BF16)
    r = x - hi.astype(F32)
    mid = r.astype(BF16)
    lo = (r - mid.astype(F32)).astype(BF16)
    return hi, mid, lo


AUG_PARTS = 3


def _data_half(head):
    return head % HEADS_PER_LANE_BLOCK


def _aug_base(head):
    return HEAD_DIM * (1 - _data_half(head))


def _fox_aug_tables():
    width = N_HEADS * LANES
    place = np.zeros((LANES, width), np.float32)
    const = np.zeros((1, width), np.float32)
    for head in range(N_HEADS):
        q_base = head * LANES + _aug_base(head)
        k_base = head * LANES + (_aug_base(head) + HEAD_DIM) % LANES
        for part in range(AUG_PARTS):
            place[part * N_HEADS + head, q_base + part] = 1.0
            place[part * N_HEADS + head, k_base + AUG_PARTS + part] = -1.0
            const[0, q_base + AUG_PARTS + part] = 1.0
            const[0, k_base + part] = 1.0
    return jnp.asarray(place, BF16), jnp.asarray(const, F32)


def _fox_proj_kernel(h_ref, g_ref, wqkv_ref, wvt_ref, wf_ref, bf_ref, place_ref, const_ref,
                     q_ref, k_ref, vt_ref, carry_ref):
    @pl.when(pl.program_id(1) == 0)
    def _():
        carry_ref[...] = jnp.zeros_like(carry_ref)

    tm = h_ref.shape[1]
    xn = _rmsnorm(h_ref[0], g_ref[...]).astype(BF16)
    lane = lax.broadcasted_iota(jnp.int32, (tm, LANES), 1)

    def project(which):
        return jnp.dot(xn, wqkv_ref[:, which * D_MODEL:(which + 1) * D_MODEL],
                       preferred_element_type=F32)

    def head_blocks(proj, fill):
        for head in range(N_HEADS):
            src = (head // HEADS_PER_LANE_BLOCK) * LANES
            dst = slice(head * LANES, (head + 1) * LANES)
            yield dst, jnp.where(_lane_half_mask(lane, _data_half(head)),
                                 proj[:, src:src + LANES], fill(head, dst)).astype(BF16)

    def by_part(parts):
        out = jnp.zeros((tm, LANES), F32)
        for p in reversed(range(AUG_PARTS)):
            out = jnp.where(lane < (p + 1) * N_HEADS, parts[p], out)
        return out.astype(BF16)


    f_logit = jnp.dot(xn, wf_ref[...], preferred_element_type=F32) + bf_ref[...]
    v_t = lax.dot_general(wvt_ref[...], xn, (((1,), (1,)), ((), ())),
                          preferred_element_type=F32)
    log_f = jnp.minimum(f_logit, 0.0) - jnp.log1p(jnp.exp(-jnp.abs(f_logit)))
    parts = by_part([p.astype(F32) for p in _split_bf16x3(log_f)])
    t_out = lax.broadcasted_iota(jnp.int32, (tm, tm), 0)
    t_in = lax.broadcasted_iota(jnp.int32, (tm, tm), 1)
    tri = (t_in <= t_out).astype(BF16)
    sums = jnp.dot(tri, parts, preferred_element_type=F32)
    q_proj = project(0) * (QK_SCALE * LOG2E)
    ones_row = (lax.broadcasted_iota(jnp.int32, (LANES - HEAD_DIM, tm), 0) == 0).astype(BF16)
    for head in range(N_HEADS):
        vt_ref[0, head * LANES:head * LANES + HEAD_DIM, :] = (
            v_t[head * HEAD_DIM:(head + 1) * HEAD_DIM].astype(BF16))
        vt_ref[0, head * LANES + HEAD_DIM:(head + 1) * LANES, :] = ones_row

    c = sums + carry_ref[...]
    for p in range(1, AUG_PARTS):
        c = c + pltpu.roll(sums, LANES - p * N_HEADS, axis=1)
    carry_ref[...] = c[tm - 1:tm, :]
    c_parts = [p.astype(F32) for p in _split_bf16x3(c * LOG2E)]
    c_parts = by_part([c_parts[0]] + [pltpu.roll(c_parts[p], p * N_HEADS, axis=1)
                                      for p in range(1, AUG_PARTS)])
    aug = jnp.dot(c_parts, place_ref[...], preferred_element_type=F32) + const_ref[...]
    k_proj = project(1)
    for dst, blk in head_blocks(q_proj, lambda head, dst: aug[:, dst]):
        q_ref[0, :, dst] = blk
    for dst, blk in head_blocks(k_proj, lambda head, dst: pltpu.roll(aug[:, dst], HEAD_DIM,
                                                                     axis=1)):
        k_ref[0, :, dst] = blk


def _fox_proj(h, g, w_in, w_qkv, b_forget):
    b, s, d = h.shape
    pad = LANES - AUG_PARTS * N_HEADS
    w_f = jnp.pad(jnp.tile(w_in[:, 3 * d:], (1, AUG_PARTS)), ((0, 0), (0, pad))).astype(BF16)
    b_f = jnp.pad(jnp.tile(b_forget.astype(F32), AUG_PARTS), (0, pad)).reshape(1, LANES)
    w_vt = w_qkv[:, 2 * d:].T
    place, const = _fox_aug_tables()
    tm = TOKEN_TILE
    width = N_HEADS * LANES
    out_spec = pl.BlockSpec((1, tm, width), lambda i, j: (i, j, 0))
    out_shape = jax.ShapeDtypeStruct((b, s, width), BF16)
    return pl.pallas_call(
        _fox_proj_kernel,
        grid=(b, s // tm),
        in_specs=[pl.BlockSpec((1, tm, d), lambda i, j: (i, j, 0)),
                  _resident((1, d)), _resident(w_qkv.shape), _resident(w_vt.shape),
                  _resident(w_f.shape), _resident(b_f.shape), _resident(place.shape),
                  _resident(const.shape)],
        out_specs=[out_spec, out_spec, pl.BlockSpec((1, width, tm), lambda i, j: (i, 0, j))],
        out_shape=[out_shape, out_shape, jax.ShapeDtypeStruct((b, width, s), BF16)],
        scratch_shapes=[pltpu.VMEM((1, LANES), F32)],
        compiler_params=_params("parallel", "arbitrary"),
        name="fox_proj",
    )(h, g.reshape(1, d), w_qkv, w_vt, w_f, b_f, place, const)


def _fox_attn_kernel(q_ref, k_ref, vt_ref, o_ref):
    seq = q_ref.shape[1]
    t = FOX_TILE
    half = t // 2
    heads = range(FOX_HEADS_PER_STEP)
    first_mask = (lax.broadcasted_iota(jnp.int32, (half, half), 0)
                  <= lax.broadcasted_iota(jnp.int32, (half, half), 1))
    second_mask = (lax.broadcasted_iota(jnp.int32, (t, half), 0)
                   <= lax.broadcasted_iota(jnp.int32, (t, half), 1) + half)

    cols = [slice(hh * LANES, (hh + 1) * LANES) for hh in heads]

    jobs = []
    for qi in range(seq // t):
        q0 = qi * t
        for j in range(qi):
            jobs += [(qi, hh, (q0, t), (j * t, t), None) for hh in heads]
        for hh in heads:
            jobs.append((qi, hh, (q0, half), (q0, half), first_mask))
            jobs.append((qi, hh, (q0 + half, half), (q0, t), second_mask))

    state = {}
    finished = {}

    def scores(job):
        _, hh, (r0, rn), (k0, kn), mask = job
        s = lax.dot_general(k_ref[0, k0:k0 + kn, cols[hh]], q_ref[0, r0:r0 + rn, cols[hh]],
                            (((1,), (1,)), ((), ())), preferred_element_type=F32)
        return s if mask is None else jnp.where(mask, s, NEG_INF)

    def absorb(job, s):
        qi, hh, (r0, rn), (k0, kn), mask = job
        m, acc = state.get((qi, hh), (jnp.full((1, t), NEG_INF, F32),
                                      jnp.zeros((LANES, t), F32)))
        if mask is not None:
            lo = r0 - qi * t
            m, acc = m[:, lo:lo + rn], acc[:, lo:lo + rn]
        m_new = jnp.maximum(m, jnp.max(s, axis=0, keepdims=True))
        p = jnp.exp2(s - m_new).astype(BF16)
        acc = jnp.exp2(m - m_new) * acc + jnp.dot(vt_ref[0, cols[hh], k0:k0 + kn], p,
                                                  preferred_element_type=F32)
        if mask is None:
            state[(qi, hh)] = (m_new, acc)
            return
        pair = hh // HEADS_PER_LANE_BLOCK
        accs = finished.setdefault((r0, pair), [])
        accs.append(acc)
        if len(accs) == HEADS_PER_LANE_BLOCK:
            o_t = jnp.concatenate([a[:HEAD_DIM] / a[HEAD_DIM:HEAD_DIM + 1] for a in accs],
                                  axis=0)
            o_ref[0, r0:r0 + rn, cols[pair]] = o_t.T.astype(BF16)

    pending = []
    for job in jobs:
        pending.append((job, scores(job)))
        if len(pending) > FOX_SCORES_AHEAD:
            absorb(*pending.pop(0))
    for item in pending:
        absorb(*item)


def _fox_attn(q, k, v_t):
    b, s, _ = q.shape
    in_w = FOX_HEADS_PER_STEP * LANES
    blk = pl.BlockSpec((1, s, in_w), lambda i, p: (i, 0, p))
    return pl.pallas_call(
        _fox_attn_kernel,
        grid=(b, N_HEADS // FOX_HEADS_PER_STEP),
        in_specs=[blk, blk, pl.BlockSpec((1, in_w, s), lambda i, p: (i, p, 0))],
        out_specs=pl.BlockSpec((1, s, FOX_HEADS_PER_STEP * HEAD_DIM), lambda i, p: (i, 0, p)),
        out_shape=jax.ShapeDtypeStruct((b, s, D_MODEL), BF16),
        compiler_params=_params("parallel", "parallel"),
        name="fox_attn",
    )(q, k, v_t)


def _swa_proj_kernel(h_ref, g_ref, w_ref, q_ref, k_ref, v_ref):
    tm = h_ref.shape[0]
    xn = _rmsnorm(h_ref[...], g_ref[...]).astype(BF16)
    qkv = jnp.dot(xn, w_ref[...], preferred_element_type=F32)
    kv_w = SWA_KV_HEADS * HEAD_DIM
    k_ref[...] = qkv[:, D_MODEL:D_MODEL + kv_w].astype(BF16)
    v_ref[...] = qkv[:, D_MODEL + kv_w:].astype(BF16)
    lane = lax.broadcasted_iota(jnp.int32, (tm, LANES), 1)
    for head in range(N_HEADS):
        kv = head // SWA_GROUP
        src = (head // HEADS_PER_LANE_BLOCK) * LANES
        q = qkv[:, src:src + LANES] * (QK_SCALE * LOG2E)
        if head % HEADS_PER_LANE_BLOCK != kv:
            q = pltpu.roll(q, HEAD_DIM, axis=1)
        q_ref[:, head * LANES:(head + 1) * LANES] = jnp.where(
            _lane_half_mask(lane, kv), q, 0.0).astype(BF16)


def _swa_proj(h, g, w_in):
    t, d = h.shape
    kv_w = SWA_KV_HEADS * HEAD_DIM
    tm = SWA_PROJ_TILE
    return pl.pallas_call(
        _swa_proj_kernel,
        grid=(t // tm,),
        in_specs=[pl.BlockSpec((tm, d), lambda i: (i, 0)), _resident((1, d)),
                  _resident(w_in.shape)],
        out_specs=[pl.BlockSpec((tm, N_HEADS * LANES), lambda i: (i, 0)),
                   pl.BlockSpec((tm, kv_w), lambda i: (i, 0)),
                   pl.BlockSpec((tm, kv_w), lambda i: (i, 0))],
        out_shape=[jax.ShapeDtypeStruct((t, N_HEADS * LANES), BF16),
                   jax.ShapeDtypeStruct((t, kv_w), BF16),
                   jax.ShapeDtypeStruct((t, kv_w), BF16)],
        compiler_params=_params("parallel"),
        name="swa_proj",
    )(h, g.reshape(1, d), w_in.astype(BF16))


def _alibi_slope(head):
    return float(np.float32(2.0 ** (-8.0 * (head + 1) / N_HEADS)))


def _swa_attn_kernel(sinks_ref, q_ref, kp_ref, kc_ref, vp_ref, vc_ref, o_ref, bias_ref):
    w = WINDOW
    step = pl.program_id(1)

    @pl.when((pl.program_id(0) == 0) & (step == 0))
    def _():
        qi = lax.broadcasted_iota(jnp.int32, (w, 2 * w), 0)
        kj = lax.broadcasted_iota(jnp.int32, (w, 2 * w), 1)
        dist = qi + w - kj
        valid = (dist >= 0) & (dist < w)
        dist_f = dist.astype(F32)
        for head in range(N_HEADS):
            sink = sinks_ref[head] * LOG2E
            bias = jnp.where(valid, -_alibi_slope(head) * dist_f * LOG2E, NEG_INF)
            bias_ref[1, head] = jnp.where(kj == 0, sink, bias)
            bias_ref[0, head] = jnp.where(kj == 0, sink, jnp.where(kj >= w, bias, NEG_INF))

    lane = lax.broadcasted_iota(jnp.int32, (w, LANES), 1)
    k_all = jnp.concatenate([kp_ref[0], kc_ref[0]], axis=0)
    v_all = jnp.concatenate([vp_ref[0], vc_ref[0]], axis=0)
    v_lane = lax.broadcasted_iota(jnp.int32, v_all.shape, 1)
    v_swapped = pltpu.roll(v_all, HEAD_DIM, axis=1)
    ones = jnp.ones_like(v_all)
    v_aug = []
    for kv in range(SWA_KV_HEADS):
        v_lo, v_hi = (v_all, v_swapped) if kv == 0 else (v_swapped, v_all)
        v_aug.append(jnp.concatenate([jnp.where(v_lane < HEAD_DIM, v_lo, ones),
                                      jnp.where(v_lane < HEAD_DIM, ones, v_hi)], axis=1))
    band_row = lax.broadcasted_iota(jnp.int32, (2 * w, 1), 0)
    sum_lanes = lax.broadcasted_iota(jnp.int32, (1, 2 * LANES), 1) // HEAD_DIM
    sum_lanes = ((sum_lanes == 1) | (sum_lanes == 2)).astype(BF16)
    k_bands, v_bands = {}, {}

    def k_band(i):
        if i not in k_bands:
            band = k_all[i * w:(i + 2) * w]
            k_bands[i] = jnp.where(band_row == 0, jnp.zeros_like(band), band)
        return k_bands[i]

    def v_band(i, kv):
        if (i, kv) not in v_bands:
            v_bands[(i, kv)] = jnp.where(band_row == 0, sum_lanes,
                                         v_aug[kv][i * w:(i + 2) * w])
        return v_bands[(i, kv)]

    n_h = SWA_HEADS_PER_JOB
    jobs = [(i, h0) for i in range(SWA_BLOCKS_PER_STEP) for h0 in range(0, N_HEADS, n_h)]

    def scores(job):
        i, h0 = job
        qx = jnp.concatenate(
            [q_ref[0, i * w:(i + 1) * w, head * LANES:(head + 1) * LANES]
             for head in range(h0, h0 + n_h)], axis=0)
        return lax.dot_general(qx, k_band(i), (((1,), (1,)), ((), ())),
                               preferred_element_type=F32)

    def absorb(job, s_all):
        i, h0 = job
        kv = h0 // SWA_GROUP
        table = jnp.minimum(step, 1) if i == 0 else 1
        p_rows = []
        for g in range(n_h):
            s = s_all[g * w:(g + 1) * w] + bias_ref[table, h0 + g]
            m = jnp.max(s, axis=-1, keepdims=True)
            p_rows.append(jnp.exp2(s - m).astype(BF16))
        acc = jnp.dot(jnp.concatenate(p_rows, axis=0), v_band(i, kv),
                      preferred_element_type=F32)
        for pair in range(n_h // HEADS_PER_LANE_BLOCK):
            num, den = [], []
            for d in range(HEADS_PER_LANE_BLOCK):
                g = pair * HEADS_PER_LANE_BLOCK + d
                a = acc[g * w:(g + 1) * w]
                num.append(a[:, d * LANES:(d + 1) * LANES])
                den.append(a[:, (1 - d) * LANES:(2 - d) * LANES])
            blk = h0 // HEADS_PER_LANE_BLOCK + pair
            o_ref[0, i * w:(i + 1) * w, blk * LANES:(blk + 1) * LANES] = (
                jnp.where(lane < HEAD_DIM, num[0], num[1])
                / jnp.where(lane < HEAD_DIM, den[0], den[1])).astype(BF16)

    pending = []
    for job in jobs:
        pending.append((job, scores(job)))
        if len(pending) > SWA_SCORES_AHEAD:
            absorb(*pending.pop(0))
    for item in pending:
        absorb(*item)


def _swa_attn(q, k, v, sinks):
    b, s, _ = q.shape
    w = WINDOW
    n = SWA_BLOCKS_PER_STEP
    kv_w = SWA_KV_HEADS * HEAD_DIM
    prev = pl.BlockSpec((1, w, kv_w), lambda i, j, *_: (i, jnp.maximum(j * n - 1, 0), 0))
    cur = pl.BlockSpec((1, n * w, kv_w), lambda i, j, *_: (i, j, 0))
    grid_spec = pltpu.PrefetchScalarGridSpec(
        num_scalar_prefetch=1,
        grid=(b, s // (n * w)),
        in_specs=[pl.BlockSpec((1, n * w, N_HEADS * LANES), lambda i, j, *_: (i, j, 0)),
                  prev, cur, prev, cur],
        out_specs=pl.BlockSpec((1, n * w, D_MODEL), lambda i, j, *_: (i, j, 0)),
        scratch_shapes=[pltpu.VMEM((2, N_HEADS, w, 2 * w), F32)],
    )
    return pl.pallas_call(
        _swa_attn_kernel,
        grid_spec=grid_spec,
        out_shape=jax.ShapeDtypeStruct((b, s, D_MODEL), BF16),
        compiler_params=_params("arbitrary", "arbitrary"),
        name="swa_attn",
    )(sinks.astype(F32), q, k, k, v, v)


def kernel(x, l0_ffn1_norm, l0_ffn1_w_gate, l0_ffn1_w_up, l0_ffn1_w_down, l0_mix_norm, l0_fox_w_in, l0_fox_b_forget, l0_fox_w_out, l0_ffn2_norm, l0_ffn2_w_gate, l0_ffn2_w_up, l0_ffn2_w_down, l1_ffn1_norm, l1_ffn1_w_gate, l1_ffn1_w_up, l1_ffn1_w_down, l1_mix_norm, l1_swa_w_in, l1_swa_sinks, l1_swa_w_out, l1_ffn2_norm, l1_ffn2_w_gate, l1_ffn2_w_up, l1_ffn2_w_down, final_norm):
    b, s, d = x.shape
    t = b * s
    h = x.reshape(t, d)

    def whole(*weights):
        return [(w, w.shape[1]) for w in weights]


    h, (fox_w_qkv, fox_w_out, wg, wu, wd) = _ffn(
        h, l0_ffn1_norm, l0_ffn1_w_gate, l0_ffn1_w_up, l0_ffn1_w_down,
        casts=[(l0_fox_w_in, 3 * d)] + whole(l0_fox_w_out, l0_ffn2_w_gate, l0_ffn2_w_up,
                                             l0_ffn2_w_down))
    q, k, v_t = _fox_proj(h.reshape(b, s, d), l0_mix_norm, l0_fox_w_in, fox_w_qkv,
                          l0_fox_b_forget)
    o = _fox_attn(q, k, v_t)
    h, (wg, wu, wd) = _ffn(h, l0_ffn2_norm, wg, wu, wd, attn=o.reshape(t, d), w_out=fox_w_out,
                           casts=whole(l1_ffn1_w_gate, l1_ffn1_w_up, l1_ffn1_w_down))

    h, (swa_w_in, swa_w_out, wg2, wu2, wd2) = _ffn(
        h, l1_ffn1_norm, wg, wu, wd,
        casts=whole(l1_swa_w_in, l1_swa_w_out, l1_ffn2_w_gate, l1_ffn2_w_up, l1_ffn2_w_down))
    q, k, v = _swa_proj(h, l1_mix_norm, swa_w_in)
    kv_w = SWA_KV_HEADS * HEAD_DIM
    o = _swa_attn(q.reshape(b, s, N_HEADS * LANES), k.reshape(b, s, kv_w),
                  v.reshape(b, s, kv_w), l1_swa_sinks)
    h, _ = _ffn(h, l1_ffn2_norm, wg2, wu2, wd2, attn=o.reshape(t, d), w_out=swa_w_out,
                final_g=final_norm)
    return h.reshape(b, s, d)
```

```python
import functools

import jax
import jax.numpy as jnp
import numpy as np
from jax import lax
from jax.experimental import pallas as pl
from jax.experimental.pallas import tpu as pltpu

F32 = jnp.float32
BF16 = jnp.bfloat16

D_MODEL = 1024
HEAD_DIM = 64
N_HEADS = D_MODEL // HEAD_DIM
SWA_KV_HEADS = 2
SWA_GROUP = N_HEADS // SWA_KV_HEADS
WINDOW = 128
RMS_EPS = 1e-6
NEG_INF = -1e30
QK_SCALE = HEAD_DIM ** -0.5
LOG2E = float(np.log2(np.e))

LANES = 128
BF16_TILE_ROWS = 16
CAST_BLOCK_ROWS = 128
HEADS_PER_LANE_BLOCK = LANES // HEAD_DIM
VMEM_LIMIT_BYTES = 56 * 1024 * 1024

TOKEN_TILE = 512
FFN_TILE = 1024
FFN_CHUNK = 512
FOX_TILE = 512
FOX_HEADS_PER_STEP = 4
FOX_SCORES_AHEAD = 4
SWA_PROJ_TILE = 1024
SWA_PROJ_SLOTS = 3
SWA_BLOCKS_PER_STEP = 16
SWA_HEADS_PER_JOB = 4
SWA_SCORES_AHEAD = 1


def _rmsnorm(x, g):
    ms = jnp.mean(x * x, axis=-1, keepdims=True)
    return x * lax.rsqrt(ms + RMS_EPS) * g


def _lane_half_mask(lane, half):
    return (lane >= half * HEAD_DIM) & (lane < (half + 1) * HEAD_DIM)


def _resident(shape):
    zeros = (0,) * len(shape)
    return pl.BlockSpec(shape, lambda *_: zeros, pipeline_mode=pl.Buffered(1))


def _params(*semantics):
    return pltpu.CompilerParams(dimension_semantics=semantics,
                                vmem_limit_bytes=VMEM_LIMIT_BYTES)


def _ffn_kernel(*refs, d_ff, mixer_out, final_norm, n_casts):
    refs = list(refs)
    h_ref = refs.pop(0)
    attn_ref, wo_ref = (refs.pop(0), refs.pop(0)) if mixer_out else (None, None)
    g_ref, wg_ref, wu_ref, wd_ref = (refs.pop(0) for _ in range(4))
    gf_ref = refs.pop(0) if final_norm else None
    cast_in = [refs.pop(0) for _ in range(n_casts)]
    o_ref = refs.pop(0)
    cast_out = [refs.pop(0) for _ in range(n_casts)]
    a_ref, = refs
    for src, dst in zip(cast_in, cast_out):
        dst[...] = src[...].astype(BF16)
    x = h_ref[...]
    if mixer_out:
        x = x + jnp.dot(attn_ref[...], wo_ref[...], preferred_element_type=F32)
    xn = _rmsnorm(x, g_ref[...]).astype(BF16)
    for c0 in range(0, d_ff, FFN_CHUNK):
        cw = min(FFN_CHUNK, d_ff - c0)
        gate = jnp.dot(xn, wg_ref[:, c0:c0 + cw], preferred_element_type=F32)
        up = jnp.dot(xn, wu_ref[:, c0:c0 + cw], preferred_element_type=F32)
        a_ref[:, c0:c0 + cw] = (gate * jax.nn.sigmoid(gate) * up).astype(BF16)
    y = jnp.dot(a_ref[...], wd_ref[...], preferred_element_type=F32)
    out = x + 0.5 * y
    if final_norm:
        out = _rmsnorm(out, gf_ref[...])
    o_ref[...] = out


def _cast_spec(rows, cols, n_steps):
    block = rows // n_steps
    if rows % n_steps or block % BF16_TILE_ROWS:
        block = CAST_BLOCK_ROWS
    last = rows // block - 1
    return pl.BlockSpec((block, cols), lambda i: (jnp.minimum(i, last), 0))


def _ffn(h, g, wg, wu, wd, attn=None, w_out=None, final_g=None, casts=()):
    t, d = h.shape
    d_ff = wg.shape[1]
    n_steps = t // FFN_TILE
    tile = pl.BlockSpec((FFN_TILE, d), lambda i: (i, 0))
    in_specs, args = [tile], [h]
    if attn is not None:
        in_specs += [pl.BlockSpec((FFN_TILE, attn.shape[1]), lambda i: (i, 0)),
                     _resident(w_out.shape)]
        args += [attn, w_out.astype(BF16)]
    in_specs += [_resident((1, d)), _resident((d, d_ff)), _resident((d, d_ff)),
                 _resident((d_ff, d))]
    args += [g.reshape(1, d), wg.astype(BF16), wu.astype(BF16), wd.astype(BF16)]
    if final_g is not None:
        in_specs.append(_resident((1, d)))
        args.append(final_g.reshape(1, d))
    cast_specs = [_cast_spec(w.shape[0], cols, n_steps) for w, cols in casts]
    outs = pl.pallas_call(
        functools.partial(_ffn_kernel, d_ff=d_ff, mixer_out=attn is not None,
                          final_norm=final_g is not None, n_casts=len(casts)),
        grid=(n_steps,),
        in_specs=in_specs + cast_specs,
        out_specs=[tile] + cast_specs,
        out_shape=[jax.ShapeDtypeStruct((t, d), F32)]
        + [jax.ShapeDtypeStruct((w.shape[0], cols), BF16) for w, cols in casts],
        scratch_shapes=[pltpu.VMEM((FFN_TILE, d_ff), BF16)],
        compiler_params=_params("arbitrary"),
        name="ffn",
    )(*args, *(w for w, _ in casts))
    return outs[0], outs[1:]


def _split_bf16x3(x):
    hi = x.astype(BF16)
    r = x - hi.astype(F32)
    mid = r.astype(BF16)
    lo = (r - mid.astype(F32)).astype(BF16)
    return hi, mid, lo


AUG_PARTS = 3


def _data_half(head):
    return head % HEADS_PER_LANE_BLOCK


def _aug_base(head):
    return HEAD_DIM * (1 - _data_half(head))


def _fox_aug_tables():
    width = N_HEADS * LANES
    place = np.zeros((LANES, width), np.float32)
    const = np.zeros((1, width), np.float32)
    for head in range(N_HEADS):
        q_base = head * LANES + _aug_base(head)
        k_base = head * LANES + (_aug_base(head) + HEAD_DIM) % LANES
        for part in range(AUG_PARTS):
            place[part * N_HEADS + head, q_base + part] = 1.0
            place[part * N_HEADS + head, k_base + AUG_PARTS + part] = -1.0
            const[0, q_base + AUG_PARTS + part] = 1.0
            const[0, k_base + part] = 1.0
    return jnp.asarray(place, BF16), jnp.asarray(const, F32)


def _fox_proj_kernel(h_ref, g_ref, wqkv_ref, wvt_ref, wf_ref, bf_ref, place_ref, const_ref,
                     q_ref, k_ref, vt_ref, carry_ref):
    @pl.when(pl.program_id(1) == 0)
    def _():
        carry_ref[...] = jnp.zeros_like(carry_ref)

    tm = h_ref.shape[1]
    xn = _rmsnorm(h_ref[0], g_ref[...]).astype(BF16)
    lane = lax.broadcasted_iota(jnp.int32, (tm, LANES), 1)

    def project(which):
        return jnp.dot(xn, wqkv_ref[:, which * D_MODEL:(which + 1) * D_MODEL],
                       preferred_element_type=F32)

    def head_blocks(proj, fill):
        for head in range(N_HEADS):
            src = (head // HEADS_PER_LANE_BLOCK) * LANES
            dst = slice(head * LANES, (head + 1) * LANES)
            yield dst, jnp.where(_lane_half_mask(lane, _data_half(head)),
                                 proj[:, src:src + LANES], fill(head, dst)).astype(BF16)

    def by_part(parts):
        out = jnp.zeros((tm, LANES), F32)
        for p in reversed(range(AUG_PARTS)):
            out = jnp.where(lane < (p + 1) * N_HEADS, parts[p], out)
        return out.astype(BF16)


    f_logit = jnp.dot(xn, wf_ref[...], preferred_element_type=F32) + bf_ref[...]
    v_t = lax.dot_general(wvt_ref[...], xn, (((1,), (1,)), ((), ())),
                          preferred_element_type=F32)
    log_f = jnp.minimum(f_logit, 0.0) - jnp.log1p(jnp.exp(-jnp.abs(f_logit)))
    parts = by_part([p.astype(F32) for p in _split_bf16x3(log_f)])
    t_out = lax.broadcasted_iota(jnp.int32, (tm, tm), 0)
    t_in = lax.broadcasted_iota(jnp.int32, (tm, tm), 1)
    tri = (t_in <= t_out).astype(BF16)
    sums = jnp.dot(tri, parts, preferred_element_type=F32)
    q_proj = project(0) * (QK_SCALE * LOG2E)
    ones_row = (lax.broadcasted_iota(jnp.int32, (LANES - HEAD_DIM, tm), 0) == 0).astype(BF16)
    for head in range(N_HEADS):
        vt_ref[0, head * LANES:head * LANES + HEAD_DIM, :] = (
            v_t[head * HEAD_DIM:(head + 1) * HEAD_DIM].astype(BF16))
        vt_ref[0, head * LANES + HEAD_DIM:(head + 1) * LANES, :] = ones_row

    c = sums + carry_ref[...]
    for p in range(1, AUG_PARTS):
        c = c + pltpu.roll(sums, LANES - p * N_HEADS, axis=1)
    carry_ref[...] = c[tm - 1:tm, :]
    c_parts = [p.astype(F32) for p in _split_bf16x3(c * LOG2E)]
    c_parts = by_part([c_parts[0]] + [pltpu.roll(c_parts[p], p * N_HEADS, axis=1)
                                      for p in range(1, AUG_PARTS)])
    aug = jnp.dot(c_parts, place_ref[...], preferred_element_type=F32) + const_ref[...]
    k_proj = project(1)
    for dst, blk in head_blocks(q_proj, lambda head, dst: aug[:, dst]):
        q_ref[0, :, dst] = blk
    for dst, blk in head_blocks(k_proj, lambda head, dst: pltpu.roll(aug[:, dst], HEAD_DIM,
                                                                     axis=1)):
        k_ref[0, :, dst] = blk


def _fox_proj(h, g, w_in, w_qkv, b_forget):
    b, s, d = h.shape
    pad = LANES - AUG_PARTS * N_HEADS
    w_f = jnp.pad(jnp.tile(w_in[:, 3 * d:], (1, AUG_PARTS)), ((0, 0), (0, pad))).astype(BF16)
    b_f = jnp.pad(jnp.tile(b_forget.astype(F32), AUG_PARTS), (0, pad)).reshape(1, LANES)
    w_vt = w_qkv[:, 2 * d:].T
    place, const = _fox_aug_tables()
    tm = TOKEN_TILE
    width = N_HEADS * LANES
    out_spec = pl.BlockSpec((1, tm, width), lambda i, j: (i, j, 0))
    out_shape = jax.ShapeDtypeStruct((b, s, width), BF16)
    return pl.pallas_call(
        _fox_proj_kernel,
        grid=(b, s // tm),
        in_specs=[pl.BlockSpec((1, tm, d), lambda i, j: (i, j, 0)),
                  _resident((1, d)), _resident(w_qkv.shape), _resident(w_vt.shape),
                  _resident(w_f.shape), _resident(b_f.shape), _resident(place.shape),
                  _resident(const.shape)],
        out_specs=[out_spec, out_spec, pl.BlockSpec((1, width, tm), lambda i, j: (i, 0, j))],
        out_shape=[out_shape, out_shape, jax.ShapeDtypeStruct((b, width, s), BF16)],
        scratch_shapes=[pltpu.VMEM((1, LANES), F32)],
        compiler_params=_params("parallel", "arbitrary"),
        name="fox_proj",
    )(h, g.reshape(1, d), w_qkv, w_vt, w_f, b_f, place, const)


def _fox_attn_kernel(q_ref, k_ref, vt_ref, o_ref):
    seq = q_ref.shape[1]
    t = FOX_TILE
    half = t // 2
    heads = range(FOX_HEADS_PER_STEP)
    first_mask = (lax.broadcasted_iota(jnp.int32, (half, half), 0)
                  <= lax.broadcasted_iota(jnp.int32, (half, half), 1))
    second_mask = (lax.broadcasted_iota(jnp.int32, (t, half), 0)
                   <= lax.broadcasted_iota(jnp.int32, (t, half), 1) + half)

    cols = [slice(hh * LANES, (hh + 1) * LANES) for hh in heads]

    jobs = []
    for qi in range(seq // t):
        q0 = qi * t
        for j in range(qi):
            jobs += [(qi, hh, (q0, t), (j * t, t), None) for hh in heads]
        for hh in heads:
            jobs.append((qi, hh, (q0, half), (q0, half), first_mask))
            jobs.append((qi, hh, (q0 + half, half), (q0, t), second_mask))

    state = {}
    finished = {}

    def scores(job):
        _, hh, (r0, rn), (k0, kn), mask = job
        s = lax.dot_general(k_ref[0, k0:k0 + kn, cols[hh]], q_ref[0, r0:r0 + rn, cols[hh]],
                            (((1,), (1,)), ((), ())), preferred_element_type=F32)
        return s if mask is None else jnp.where(mask, s, NEG_INF)

    def absorb(job, s):
        qi, hh, (r0, rn), (k0, kn), mask = job
        m, acc = state.get((qi, hh), (jnp.full((1, t), NEG_INF, F32),
                                      jnp.zeros((LANES, t), F32)))
        if mask is not None:
            lo = r0 - qi * t
            m, acc = m[:, lo:lo + rn], acc[:, lo:lo + rn]
        m_new = jnp.maximum(m, jnp.max(s, axis=0, keepdims=True))
        p = jnp.exp2(s - m_new).astype(BF16)
        acc = jnp.exp2(m - m_new) * acc + jnp.dot(vt_ref[0, cols[hh], k0:k0 + kn], p,
                                                  preferred_element_type=F32)
        if mask is None:
            state[(qi, hh)] = (m_new, acc)
            return
        pair = hh // HEADS_PER_LANE_BLOCK
        accs = finished.setdefault((r0, pair), [])
        accs.append(acc)
        if len(accs) == HEADS_PER_LANE_BLOCK:
            o_t = jnp.concatenate([a[:HEAD_DIM] / a[HEAD_DIM:HEAD_DIM + 1] for a in accs],
                                  axis=0)
            o_ref[0, r0:r0 + rn, cols[pair]] = o_t.T.astype(BF16)

    pending = []
    for job in jobs:
        pending.append((job, scores(job)))
        if len(pending) > FOX_SCORES_AHEAD:
            absorb(*pending.pop(0))
    for item in pending:
        absorb(*item)


def _fox_attn(q, k, v_t):
    b, s, _ = q.shape
    in_w = FOX_HEADS_PER_STEP * LANES
    blk = pl.BlockSpec((1, s, in_w), lambda i, p: (i, 0, p))
    return pl.pallas_call(
        _fox_attn_kernel,
        grid=(b, N_HEADS // FOX_HEADS_PER_STEP),
        in_specs=[blk, blk, pl.BlockSpec((1, in_w, s), lambda i, p: (i, p, 0))],
        out_specs=pl.BlockSpec((1, s, FOX_HEADS_PER_STEP * HEAD_DIM), lambda i, p: (i, 0, p)),
        out_shape=jax.ShapeDtypeStruct((b, s, D_MODEL), BF16),
        compiler_params=_params("parallel", "parallel"),
        name="fox_attn",
    )(q, k, v_t)


def _swa_proj_kernel(h_hbm, g_ref, w_ref, q_ref, k_ref, v_ref, h_buf, h_sem):
    tm = h_buf.shape[1]
    step = pl.program_id(0)
    n_steps = pl.num_programs(0)

    def tile_copy(s):
        slot = s % SWA_PROJ_SLOTS
        return pltpu.make_async_copy(h_hbm.at[pl.ds(s * tm, tm), :], h_buf.at[slot],
                                     h_sem.at[slot])

    @pl.when(step == 0)
    def _():
        for s in range(SWA_PROJ_SLOTS - 1):
            tile_copy(s).start()

    @pl.when(step + SWA_PROJ_SLOTS - 1 < n_steps)
    def _():
        tile_copy(step + SWA_PROJ_SLOTS - 1).start()

    tile_copy(step).wait()
    xn = _rmsnorm(h_buf[step % SWA_PROJ_SLOTS], g_ref[...]).astype(BF16)
    qkv = jnp.dot(xn, w_ref[...], preferred_element_type=F32)
    kv_w = SWA_KV_HEADS * HEAD_DIM
    k_ref[...] = qkv[:, D_MODEL:D_MODEL + kv_w].astype(BF16)
    v_ref[...] = qkv[:, D_MODEL + kv_w:].astype(BF16)
    lane = lax.broadcasted_iota(jnp.int32, (tm, LANES), 1)
    for head in range(N_HEADS):
        kv = head // SWA_GROUP
        src = (head // HEADS_PER_LANE_BLOCK) * LANES
        q = qkv[:, src:src + LANES] * (QK_SCALE * LOG2E)
        if head % HEADS_PER_LANE_BLOCK != kv:
            q = pltpu.roll(q, HEAD_DIM, axis=1)
        q_ref[:, head * LANES:(head + 1) * LANES] = jnp.where(
            _lane_half_mask(lane, kv), q, 0.0).astype(BF16)


def _swa_proj(h, g, w_in):
    t, d = h.shape
    kv_w = SWA_KV_HEADS * HEAD_DIM
    tm = SWA_PROJ_TILE
    return pl.pallas_call(
        _swa_proj_kernel,
        grid=(t // tm,),
        in_specs=[pl.BlockSpec(memory_space=pl.ANY), _resident((1, d)),
                  _resident(w_in.shape)],
        out_specs=[pl.BlockSpec((tm, N_HEADS * LANES), lambda i: (i, 0)),
                   pl.BlockSpec((tm, kv_w), lambda i: (i, 0)),
                   pl.BlockSpec((tm, kv_w), lambda i: (i, 0))],
        out_shape=[jax.ShapeDtypeStruct((t, N_HEADS * LANES), BF16),
                   jax.ShapeDtypeStruct((t, kv_w), BF16),
                   jax.ShapeDtypeStruct((t, kv_w), BF16)],
        scratch_shapes=[pltpu.VMEM((SWA_PROJ_SLOTS, tm, d), F32),
                        pltpu.SemaphoreType.DMA((SWA_PROJ_SLOTS,))],
        compiler_params=_params("arbitrary"),
        name="swa_proj",
    )(h, g.reshape(1, d), w_in.astype(BF16))


def _alibi_slope(head):
    return float(np.float32(2.0 ** (-8.0 * (head + 1) / N_HEADS)))


def _swa_attn_kernel(sinks_ref, q_ref, kp_ref, kc_ref, vp_ref, vc_ref, o_ref, bias_ref):
    w = WINDOW
    step = pl.program_id(1)

    @pl.when((pl.program_id(0) == 0) & (step == 0))
    def _():
        qi = lax.broadcasted_iota(jnp.int32, (w, 2 * w), 0)
        kj = lax.broadcasted_iota(jnp.int32, (w, 2 * w), 1)
        dist = qi + w - kj
        valid = (dist >= 0) & (dist < w)
        dist_f = dist.astype(F32)
        for head in range(N_HEADS):
            sink = sinks_ref[head] * LOG2E
            bias = jnp.where(valid, -_alibi_slope(head) * dist_f * LOG2E, NEG_INF)
            bias_ref[1, head] = jnp.where(kj == 0, sink, bias)
            bias_ref[0, head] = jnp.where(kj == 0, sink, jnp.where(kj >= w, bias, NEG_INF))

    lane = lax.broadcasted_iota(jnp.int32, (w, LANES), 1)
    k_all = jnp.concatenate([kp_ref[0], kc_ref[0]], axis=0)
    v_all = jnp.concatenate([vp_ref[0], vc_ref[0]], axis=0)
    v_lane = lax.broadcasted_iota(jnp.int32, v_all.shape, 1)
    v_swapped = pltpu.roll(v_all, HEAD_DIM, axis=1)
    ones = jnp.ones_like(v_all)
    v_aug = []
    for kv in range(SWA_KV_HEADS):
        v_lo, v_hi = (v_all, v_swapped) if kv == 0 else (v_swapped, v_all)
        v_aug.append(jnp.concatenate([jnp.where(v_lane < HEAD_DIM, v_lo, ones),
                                      jnp.where(v_lane < HEAD_DIM, ones, v_hi)], axis=1))
    band_row = lax.broadcasted_iota(jnp.int32, (2 * w, 1), 0)
    sum_lanes = lax.broadcasted_iota(jnp.int32, (1, 2 * LANES), 1) // HEAD_DIM
    sum_lanes = ((sum_lanes == 1) | (sum_lanes == 2)).astype(BF16)
    k_bands, v_bands = {}, {}

    def k_band(i):
        if i not in k_bands:
            band = k_all[i * w:(i + 2) * w]
            k_bands[i] = jnp.where(band_row == 0, jnp.zeros_like(band), band)
        return k_bands[i]

    def v_band(i, kv):
        if (i, kv) not in v_bands:
            v_bands[(i, kv)] = jnp.where(band_row == 0, sum_lanes,
                                         v_aug[kv][i * w:(i + 2) * w])
        return v_bands[(i, kv)]

    n_h = SWA_HEADS_PER_JOB
    jobs = [(i, h0) for i in range(SWA_BLOCKS_PER_STEP) for h0 in range(0, N_HEADS, n_h)]

    def scores(job):
        i, h0 = job
        qx = jnp.concatenate(
            [q_ref[0, i * w:(i + 1) * w, head * LANES:(head + 1) * LANES]
             for head in range(h0, h0 + n_h)], axis=0)
        return lax.dot_general(qx, k_band(i), (((1,), (1,)), ((), ())),
                               preferred_element_type=F32)

    def absorb(job, s_all):
        i, h0 = job
        kv = h0 // SWA_GROUP
        table = jnp.minimum(step, 1) if i == 0 else 1
        p_rows = []
        for g in range(n_h):
            s = s_all[g * w:(g + 1) * w] + bias_ref[table, h0 + g]
            m = jnp.max(s, axis=-1, keepdims=True)
            p_rows.append(jnp.exp2(s - m).astype(BF16))
        acc = jnp.dot(jnp.concatenate(p_rows, axis=0), v_band(i, kv),
                      preferred_element_type=F32)
        for pair in range(n_h // HEADS_PER_LANE_BLOCK):
            num, den = [], []
            for d in range(HEADS_PER_LANE_BLOCK):
                g = pair * HEADS_PER_LANE_BLOCK + d
                a = acc[g * w:(g + 1) * w]
                num.append(a[:, d * LANES:(d + 1) * LANES])
                den.append(a[:, (1 - d) * LANES:(2 - d) * LANES])
            blk = h0 // HEADS_PER_LANE_BLOCK + pair
            o_ref[0, i * w:(i + 1) * w, blk * LANES:(blk + 1) * LANES] = (
                jnp.where(lane < HEAD_DIM, num[0], num[1])
                / jnp.where(lane < HEAD_DIM, den[0], den[1])).astype(BF16)

    pending = []
    for job in jobs:
        pending.append((job, scores(job)))
        if len(pending) > SWA_SCORES_AHEAD:
            absorb(*pending.pop(0))
    for item in pending:
        absorb(*item)


def _swa_attn(q, k, v, sinks):
    b, s, _ = q.shape
    w = WINDOW
    n = SWA_BLOCKS_PER_STEP
    kv_w = SWA_KV_HEADS * HEAD_DIM
    prev = pl.BlockSpec((1, w, kv_w), lambda i, j, *_: (i, jnp.maximum(j * n - 1, 0), 0))
    cur = pl.BlockSpec((1, n * w, kv_w), lambda i, j, *_: (i, j, 0))
    grid_spec = pltpu.PrefetchScalarGridSpec(
        num_scalar_prefetch=1,
        grid=(b, s // (n * w)),
        in_specs=[pl.BlockSpec((1, n * w, N_HEADS * LANES), lambda i, j, *_: (i, j, 0)),
                  prev, cur, prev, cur],
        out_specs=pl.BlockSpec((1, n * w, D_MODEL), lambda i, j, *_: (i, j, 0)),
        scratch_shapes=[pltpu.VMEM((2, N_HEADS, w, 2 * w), F32)],
    )
    return pl.pallas_call(
        _swa_attn_kernel,
        grid_spec=grid_spec,
        out_shape=jax.ShapeDtypeStruct((b, s, D_MODEL), BF16),
        compiler_params=_params("arbitrary", "arbitrary"),
        name="swa_attn",
    )(sinks.astype(F32), q, k, k, v, v)


def kernel(x, l0_ffn1_norm, l0_ffn1_w_gate, l0_ffn1_w_up, l0_ffn1_w_down, l0_mix_norm, l0_fox_w_in, l0_fox_b_forget, l0_fox_w_out, l0_ffn2_norm, l0_ffn2_w_gate, l0_ffn2_w_up, l0_ffn2_w_down, l1_ffn1_norm, l1_ffn1_w_gate, l1_ffn1_w_up, l1_ffn1_w_down, l1_mix_norm, l1_swa_w_in, l1_swa_sinks, l1_swa_w_out, l1_ffn2_norm, l1_ffn2_w_gate, l1_ffn2_w_up, l1_ffn2_w_down, final_norm):
    b, s, d = x.shape
    t = b * s
    h = x.reshape(t, d)

    def whole(*weights):
        return [(w, w.shape[1]) for w in weights]


    h, (fox_w_qkv, fox_w_out, wg, wu, wd) = _ffn(
        h, l0_ffn1_norm, l0_ffn1_w_gate, l0_ffn1_w_up, l0_ffn1_w_down,
        casts=[(l0_fox_w_in, 3 * d)] + whole(l0_fox_w_out, l0_ffn2_w_gate, l0_ffn2_w_up,
                                             l0_ffn2_w_down))
    q, k, v_t = _fox_proj(h.reshape(b, s, d), l0_mix_norm, l0_fox_w_in, fox_w_qkv,
                          l0_fox_b_forget)
    o = _fox_attn(q, k, v_t)
    h, (wg, wu, wd) = _ffn(h, l0_ffn2_norm, wg, wu, wd, attn=o.reshape(t, d), w_out=fox_w_out,
                           casts=whole(l1_ffn1_w_gate, l1_ffn1_w_up, l1_ffn1_w_down))

    h, (swa_w_in, swa_w_out, wg2, wu2, wd2) = _ffn(
        h, l1_ffn1_norm, wg, wu, wd,
        casts=whole(l1_swa_w_in, l1_swa_w_out, l1_ffn2_w_gate, l1_ffn2_w_up, l1_ffn2_w_down))
    q, k, v = _swa_proj(h, l1_mix_norm, swa_w_in)
    kv_w = SWA_KV_HEADS * HEAD_DIM
    o = _swa_attn(q.reshape(b, s, N_HEADS * LANES), k.reshape(b, s, kv_w),
                  v.reshape(b, s, kv_w), l1_swa_sinks)
    h, _ = _ffn(h, l1_ffn2_norm, wg2, wu2, wd2, attn=o.reshape(t, d), w_out=swa_w_out,
                final_g=final_norm)
    return h.reshape(b, s, d)
```

```python
import functools

import jax
import jax.numpy as jnp
import numpy as np
from jax import lax
from jax.experimental import pallas as pl
from jax.experimental.pallas import tpu as pltpu

F32 = jnp.float32
BF16 = jnp.bfloat16

D_MODEL = 1024
HEAD_DIM = 64
N_HEADS = D_MODEL // HEAD_DIM
SWA_KV_HEADS = 2
SWA_GROUP = N_HEADS // SWA_KV_HEADS
WINDOW = 128
RMS_EPS = 1e-6
NEG_INF = -1e30
QK_SCALE = HEAD_DIM ** -0.5
LOG2E = float(np.log2(np.e))

LANES = 128
BF16_TILE_ROWS = 16
CAST_BLOCK_ROWS = 128
HEADS_PER_LANE_BLOCK = LANES // HEAD_DIM
VMEM_LIMIT_BYTES = 56 * 1024 * 1024

TOKEN_TILE = 512
FFN_TILE = 1024
FFN_CHUNK = 512
FOX_TILE = 512
FOX_HEADS_PER_STEP = 4
FOX_SCORES_AHEAD = 4
SWA_PROJ_TILE = 1024
SWA_PROJ_SLOTS = 3
RING_DMA_PRIORITY = 1
SWA_BLOCKS_PER_STEP = 16
SWA_HEADS_PER_JOB = 4
SWA_SCORES_AHEAD = 1


def _rmsnorm(x, g):
    ms = jnp.mean(x * x, axis=-1, keepdims=True)
    return x * lax.rsqrt(ms + RMS_EPS) * g


def _lane_half_mask(lane, half):
    return (lane >= half * HEAD_DIM) & (lane < (half + 1) * HEAD_DIM)


def _resident(shape):
    zeros = (0,) * len(shape)
    return pl.BlockSpec(shape, lambda *_: zeros, pipeline_mode=pl.Buffered(1))


def _params(*semantics):
    return pltpu.CompilerParams(dimension_semantics=semantics,
                                vmem_limit_bytes=VMEM_LIMIT_BYTES)


def _ffn_kernel(*refs, d_ff, mixer_out, final_norm, n_casts):
    refs = list(refs)
    h_ref = refs.pop(0)
    attn_ref, wo_ref = (refs.pop(0), refs.pop(0)) if mixer_out else (None, None)
    g_ref, wg_ref, wu_ref, wd_ref = (refs.pop(0) for _ in range(4))
    gf_ref = refs.pop(0) if final_norm else None
    cast_in = [refs.pop(0) for _ in range(n_casts)]
    o_ref = refs.pop(0)
    cast_out = [refs.pop(0) for _ in range(n_casts)]
    a_ref, = refs
    for src, dst in zip(cast_in, cast_out):
        dst[...] = src[...].astype(BF16)
    x = h_ref[...]
    if mixer_out:
        x = x + jnp.dot(attn_ref[...], wo_ref[...], preferred_element_type=F32)
    xn = _rmsnorm(x, g_ref[...]).astype(BF16)
    for c0 in range(0, d_ff, FFN_CHUNK):
        cw = min(FFN_CHUNK, d_ff - c0)
        gate = jnp.dot(xn, wg_ref[:, c0:c0 + cw], preferred_element_type=F32)
        up = jnp.dot(xn, wu_ref[:, c0:c0 + cw], preferred_element_type=F32)
        a_ref[:, c0:c0 + cw] = (gate * jax.nn.sigmoid(gate) * up).astype(BF16)
    y = jnp.dot(a_ref[...], wd_ref[...], preferred_element_type=F32)
    out = x + 0.5 * y
    if final_norm:
        out = _rmsnorm(out, gf_ref[...])
    o_ref[...] = out


def _cast_spec(rows, cols, n_steps):
    block = rows // n_steps
    if rows % n_steps or block % BF16_TILE_ROWS:
        block = CAST_BLOCK_ROWS
    last = rows // block - 1
    return pl.BlockSpec((block, cols), lambda i: (jnp.minimum(i, last), 0))


def _ffn(h, g, wg, wu, wd, attn=None, w_out=None, final_g=None, casts=()):
    t, d = h.shape
    d_ff = wg.shape[1]
    n_steps = t // FFN_TILE
    tile = pl.BlockSpec((FFN_TILE, d), lambda i: (i, 0))
    in_specs, args = [tile], [h]
    if attn is not None:
        in_specs += [pl.BlockSpec((FFN_TILE, attn.shape[1]), lambda i: (i, 0)),
                     _resident(w_out.shape)]
        args += [attn, w_out.astype(BF16)]
    in_specs += [_resident((1, d)), _resident((d, d_ff)), _resident((d, d_ff)),
                 _resident((d_ff, d))]
    args += [g.reshape(1, d), wg.astype(BF16), wu.astype(BF16), wd.astype(BF16)]
    if final_g is not None:
        in_specs.append(_resident((1, d)))
        args.append(final_g.reshape(1, d))
    cast_specs = [_cast_spec(w.shape[0], cols, n_steps) for w, cols in casts]
    outs = pl.pallas_call(
        functools.partial(_ffn_kernel, d_ff=d_ff, mixer_out=attn is not None,
                          final_norm=final_g is not None, n_casts=len(casts)),
        grid=(n_steps,),
        in_specs=in_specs + cast_specs,
        out_specs=[tile] + cast_specs,
        out_shape=[jax.ShapeDtypeStruct((t, d), F32)]
        + [jax.ShapeDtypeStruct((w.shape[0], cols), BF16) for w, cols in casts],
        scratch_shapes=[pltpu.VMEM((FFN_TILE, d_ff), BF16)],
        compiler_params=_params("arbitrary"),
        name="ffn",
    )(*args, *(w for w, _ in casts))
    return outs[0], outs[1:]


def _split_bf16x3(x):
    hi = x.astype(BF16)
    r = x - hi.astype(F32)
    mid = r.astype(BF16)
    lo = (r - mid.astype(F32)).astype(BF16)
    return hi, mid, lo


AUG_PARTS = 3


def _data_half(head):
    return head % HEADS_PER_LANE_BLOCK


def _aug_base(head):
    return HEAD_DIM * (1 - _data_half(head))


def _fox_aug_tables():
    width = N_HEADS * LANES
    place = np.zeros((LANES, width), np.float32)
    const = np.zeros((1, width), np.float32)
    for head in range(N_HEADS):
        q_base = head * LANES + _aug_base(head)
        k_base = head * LANES + (_aug_base(head) + HEAD_DIM) % LANES
        for part in range(AUG_PARTS):
            place[part * N_HEADS + head, q_base + part] = 1.0
            place[part * N_HEADS + head, k_base + AUG_PARTS + part] = -1.0
            const[0, q_base + AUG_PARTS + part] = 1.0
            const[0, k_base + part] = 1.0
    return jnp.asarray(place, BF16), jnp.asarray(const, F32)


def _fox_proj_kernel(h_ref, g_ref, wqkv_ref, wvt_ref, wf_ref, bf_ref, place_ref, const_ref,
                     q_ref, k_ref, vt_ref, carry_ref):
    @pl.when(pl.program_id(1) == 0)
    def _():
        carry_ref[...] = jnp.zeros_like(carry_ref)

    tm = h_ref.shape[1]
    xn = _rmsnorm(h_ref[0], g_ref[...]).astype(BF16)
    lane = lax.broadcasted_iota(jnp.int32, (tm, LANES), 1)

    def project(which):
        return jnp.dot(xn, wqkv_ref[:, which * D_MODEL:(which + 1) * D_MODEL],
                       preferred_element_type=F32)

    def head_blocks(proj, fill):
        for head in range(N_HEADS):
            src = (head // HEADS_PER_LANE_BLOCK) * LANES
            dst = slice(head * LANES, (head + 1) * LANES)
            yield dst, jnp.where(_lane_half_mask(lane, _data_half(head)),
                                 proj[:, src:src + LANES], fill(head, dst)).astype(BF16)

    def by_part(parts):
        out = jnp.zeros((tm, LANES), F32)
        for p in reversed(range(AUG_PARTS)):
            out = jnp.where(lane < (p + 1) * N_HEADS, parts[p], out)
        return out.astype(BF16)


    f_logit = jnp.dot(xn, wf_ref[...], preferred_element_type=F32) + bf_ref[...]
    v_t = lax.dot_general(wvt_ref[...], xn, (((1,), (1,)), ((), ())),
                          preferred_element_type=F32)
    log_f = jnp.minimum(f_logit, 0.0) - jnp.log1p(jnp.exp(-jnp.abs(f_logit)))
    parts = by_part([p.astype(F32) for p in _split_bf16x3(log_f)])
    t_out = lax.broadcasted_iota(jnp.int32, (tm, tm), 0)
    t_in = lax.broadcasted_iota(jnp.int32, (tm, tm), 1)
    tri = (t_in <= t_out).astype(BF16)
    sums = jnp.dot(tri, parts, preferred_element_type=F32)
    q_proj = project(0) * (QK_SCALE * LOG2E)
    ones_row = (lax.broadcasted_iota(jnp.int32, (LANES - HEAD_DIM, tm), 0) == 0).astype(BF16)
    for head in range(N_HEADS):
        vt_ref[0, head * LANES:head * LANES + HEAD_DIM, :] = (
            v_t[head * HEAD_DIM:(head + 1) * HEAD_DIM].astype(BF16))
        vt_ref[0, head * LANES + HEAD_DIM:(head + 1) * LANES, :] = ones_row

    c = sums + carry_ref[...]
    for p in range(1, AUG_PARTS):
        c = c + pltpu.roll(sums, LANES - p * N_HEADS, axis=1)
    carry_ref[...] = c[tm - 1:tm, :]
    c_parts = [p.astype(F32) for p in _split_bf16x3(c * LOG2E)]
    c_parts = by_part([c_parts[0]] + [pltpu.roll(c_parts[p], p * N_HEADS, axis=1)
                                      for p in range(1, AUG_PARTS)])
    aug = jnp.dot(c_parts, place_ref[...], preferred_element_type=F32) + const_ref[...]
    k_proj = project(1)
    for dst, blk in head_blocks(q_proj, lambda head, dst: aug[:, dst]):
        q_ref[0, :, dst] = blk
    for dst, blk in head_blocks(k_proj, lambda head, dst: pltpu.roll(aug[:, dst], HEAD_DIM,
                                                                     axis=1)):
        k_ref[0, :, dst] = blk


def _fox_proj(h, g, w_in, w_qkv, b_forget):
    b, s, d = h.shape
    pad = LANES - AUG_PARTS * N_HEADS
    w_f = jnp.pad(jnp.tile(w_in[:, 3 * d:], (1, AUG_PARTS)), ((0, 0), (0, pad))).astype(BF16)
    b_f = jnp.pad(jnp.tile(b_forget.astype(F32), AUG_PARTS), (0, pad)).reshape(1, LANES)
    w_vt = w_qkv[:, 2 * d:].T
    place, const = _fox_aug_tables()
    tm = TOKEN_TILE
    width = N_HEADS * LANES
    out_spec = pl.BlockSpec((1, tm, width), lambda i, j: (i, j, 0))
    out_shape = jax.ShapeDtypeStruct((b, s, width), BF16)
    return pl.pallas_call(
        _fox_proj_kernel,
        grid=(b, s // tm),
        in_specs=[pl.BlockSpec((1, tm, d), lambda i, j: (i, j, 0)),
                  _resident((1, d)), _resident(w_qkv.shape), _resident(w_vt.shape),
                  _resident(w_f.shape), _resident(b_f.shape), _resident(place.shape),
                  _resident(const.shape)],
        out_specs=[out_spec, out_spec, pl.BlockSpec((1, width, tm), lambda i, j: (i, 0, j))],
        out_shape=[out_shape, out_shape, jax.ShapeDtypeStruct((b, width, s), BF16)],
        scratch_shapes=[pltpu.VMEM((1, LANES), F32)],
        compiler_params=_params("parallel", "arbitrary"),
        name="fox_proj",
    )(h, g.reshape(1, d), w_qkv, w_vt, w_f, b_f, place, const)


def _fox_attn_kernel(q_ref, k_ref, vt_ref, o_ref):
    seq = q_ref.shape[1]
    t = FOX_TILE
    half = t // 2
    heads = range(FOX_HEADS_PER_STEP)
    first_mask = (lax.broadcasted_iota(jnp.int32, (half, half), 0)
                  <= lax.broadcasted_iota(jnp.int32, (half, half), 1))
    second_mask = (lax.broadcasted_iota(jnp.int32, (t, half), 0)
                   <= lax.broadcasted_iota(jnp.int32, (t, half), 1) + half)

    cols = [slice(hh * LANES, (hh + 1) * LANES) for hh in heads]

    jobs = []
    for qi in range(seq // t):
        q0 = qi * t
        for j in range(qi):
            jobs += [(qi, hh, (q0, t), (j * t, t), None) for hh in heads]
        for hh in heads:
            jobs.append((qi, hh, (q0, half), (q0, half), first_mask))
            jobs.append((qi, hh, (q0 + half, half), (q0, t), second_mask))

    state = {}
    finished = {}

    def scores(job):
        _, hh, (r0, rn), (k0, kn), mask = job
        s = lax.dot_general(k_ref[0, k0:k0 + kn, cols[hh]], q_ref[0, r0:r0 + rn, cols[hh]],
                            (((1,), (1,)), ((), ())), preferred_element_type=F32)
        return s if mask is None else jnp.where(mask, s, NEG_INF)

    def absorb(job, s):
        qi, hh, (r0, rn), (k0, kn), mask = job
        m, acc = state.get((qi, hh), (jnp.full((1, t), NEG_INF, F32),
                                      jnp.zeros((LANES, t), F32)))
        if mask is not None:
            lo = r0 - qi * t
            m, acc = m[:, lo:lo + rn], acc[:, lo:lo + rn]
        m_new = jnp.maximum(m, jnp.max(s, axis=0, keepdims=True))
        p = jnp.exp2(s - m_new).astype(BF16)
        acc = jnp.exp2(m - m_new) * acc + jnp.dot(vt_ref[0, cols[hh], k0:k0 + kn], p,
                                                  preferred_element_type=F32)
        if mask is None:
            state[(qi, hh)] = (m_new, acc)
            return
        pair = hh // HEADS_PER_LANE_BLOCK
        accs = finished.setdefault((r0, pair), [])
        accs.append(acc)
        if len(accs) == HEADS_PER_LANE_BLOCK:
            o_t = jnp.concatenate([a[:HEAD_DIM] / a[HEAD_DIM:HEAD_DIM + 1] for a in accs],
                                  axis=0)
            o_ref[0, r0:r0 + rn, cols[pair]] = o_t.T.astype(BF16)

    pending = []
    for job in jobs:
        pending.append((job, scores(job)))
        if len(pending) > FOX_SCORES_AHEAD:
            absorb(*pending.pop(0))
    for item in pending:
        absorb(*item)


def _fox_attn(q, k, v_t):
    b, s, _ = q.shape
    in_w = FOX_HEADS_PER_STEP * LANES
    blk = pl.BlockSpec((1, s, in_w), lambda i, p: (i, 0, p))
    return pl.pallas_call(
        _fox_attn_kernel,
        grid=(b, N_HEADS // FOX_HEADS_PER_STEP),
        in_specs=[blk, blk, pl.BlockSpec((1, in_w, s), lambda i, p: (i, p, 0))],
        out_specs=pl.BlockSpec((1, s, FOX_HEADS_PER_STEP * HEAD_DIM), lambda i, p: (i, 0, p)),
        out_shape=jax.ShapeDtypeStruct((b, s, D_MODEL), BF16),
        compiler_params=_params("parallel", "parallel"),
        name="fox_attn",
    )(q, k, v_t)


def _swa_proj_kernel(h_hbm, g_ref, w_ref, q_ref, k_ref, v_ref, h_buf, h_sem):
    tm = h_buf.shape[1]
    step = pl.program_id(0)
    n_steps = pl.num_programs(0)

    def tile_copy(s):
        slot = s % SWA_PROJ_SLOTS
        return pltpu.make_async_copy(h_hbm.at[pl.ds(s * tm, tm), :], h_buf.at[slot],
                                     h_sem.at[slot])

    @pl.when(step == 0)
    def _():
        for s in range(SWA_PROJ_SLOTS - 1):
            tile_copy(s).start(priority=RING_DMA_PRIORITY)

    @pl.when(step + SWA_PROJ_SLOTS - 1 < n_steps)
    def _():
        tile_copy(step + SWA_PROJ_SLOTS - 1).start(priority=RING_DMA_PRIORITY)

    tile_copy(step).wait()
    xn = _rmsnorm(h_buf[step % SWA_PROJ_SLOTS], g_ref[...]).astype(BF16)
    qkv = jnp.dot(xn, w_ref[...], preferred_element_type=F32)
    kv_w = SWA_KV_HEADS * HEAD_DIM
    k_ref[...] = qkv[:, D_MODEL:D_MODEL + kv_w].astype(BF16)
    v_ref[...] = qkv[:, D_MODEL + kv_w:].astype(BF16)
    lane = lax.broadcasted_iota(jnp.int32, (tm, LANES), 1)
    for head in range(N_HEADS):
        kv = head // SWA_GROUP
        src = (head // HEADS_PER_LANE_BLOCK) * LANES
        q = qkv[:, src:src + LANES] * (QK_SCALE * LOG2E)
        if head % HEADS_PER_LANE_BLOCK != kv:
            q = pltpu.roll(q, HEAD_DIM, axis=1)
        q_ref[:, head * LANES:(head + 1) * LANES] = jnp.where(
            _lane_half_mask(lane, kv), q, 0.0).astype(BF16)


def _swa_proj(h, g, w_in):
    t, d = h.shape
    kv_w = SWA_KV_HEADS * HEAD_DIM
    tm = SWA_PROJ_TILE
    return pl.pallas_call(
        _swa_proj_kernel,
        grid=(t // tm,),
        in_specs=[pl.BlockSpec(memory_space=pl.ANY), _resident((1, d)),
                  _resident(w_in.shape)],
        out_specs=[pl.BlockSpec((tm, N_HEADS * LANES), lambda i: (i, 0)),
                   pl.BlockSpec((tm, kv_w), lambda i: (i, 0)),
                   pl.BlockSpec((tm, kv_w), lambda i: (i, 0))],
        out_shape=[jax.ShapeDtypeStruct((t, N_HEADS * LANES), BF16),
                   jax.ShapeDtypeStruct((t, kv_w), BF16),
                   jax.ShapeDtypeStruct((t, kv_w), BF16)],
        scratch_shapes=[pltpu.VMEM((SWA_PROJ_SLOTS, tm, d), F32),
                        pltpu.SemaphoreType.DMA((SWA_PROJ_SLOTS,))],
        compiler_params=_params("arbitrary"),
        name="swa_proj",
    )(h, g.reshape(1, d), w_in.astype(BF16))


def _alibi_slope(head):
    return float(np.float32(2.0 ** (-8.0 * (head + 1) / N_HEADS)))


def _swa_attn_kernel(sinks_ref, q_ref, kp_ref, kc_ref, vp_ref, vc_ref, o_ref, bias_ref):
    w = WINDOW
    step = pl.program_id(1)

    @pl.when((pl.program_id(0) == 0) & (step == 0))
    def _():
        qi = lax.broadcasted_iota(jnp.int32, (w, 2 * w), 0)
        kj = lax.broadcasted_iota(jnp.int32, (w, 2 * w), 1)
        dist = qi + w - kj
        valid = (dist >= 0) & (dist < w)
        dist_f = dist.astype(F32)
        for head in range(N_HEADS):
            sink = sinks_ref[head] * LOG2E
            bias = jnp.where(valid, -_alibi_slope(head) * dist_f * LOG2E, NEG_INF)
            bias_ref[1, head] = jnp.where(kj == 0, sink, bias)
            bias_ref[0, head] = jnp.where(kj == 0, sink, jnp.where(kj >= w, bias, NEG_INF))

    lane = lax.broadcasted_iota(jnp.int32, (w, LANES), 1)
    k_all = jnp.concatenate([kp_ref[0], kc_ref[0]], axis=0)
    v_all = jnp.concatenate([vp_ref[0], vc_ref[0]], axis=0)
    v_lane = lax.broadcasted_iota(jnp.int32, v_all.shape, 1)
    v_swapped = pltpu.roll(v_all, HEAD_DIM, axis=1)
    ones = jnp.ones_like(v_all)
    v_aug = []
    for kv in range(SWA_KV_HEADS):
        v_lo, v_hi = (v_all, v_swapped) if kv == 0 else (v_swapped, v_all)
        v_aug.append(jnp.concatenate([jnp.where(v_lane < HEAD_DIM, v_lo, ones),
                                      jnp.where(v_lane < HEAD_DIM, ones, v_hi)], axis=1))
    band_row = lax.broadcasted_iota(jnp.int32, (2 * w, 1), 0)
    sum_lanes = lax.broadcasted_iota(jnp.int32, (1, 2 * LANES), 1) // HEAD_DIM
    sum_lanes = ((sum_lanes == 1) | (sum_lanes == 2)).astype(BF16)
    k_bands, v_bands = {}, {}

    def k_band(i):
        if i not in k_bands:
            band = k_all[i * w:(i + 2) * w]
            k_bands[i] = jnp.where(band_row == 0, jnp.zeros_like(band), band)
        return k_bands[i]

    def v_band(i, kv):
        if (i, kv) not in v_bands:
            v_bands[(i, kv)] = jnp.where(band_row == 0, sum_lanes,
                                         v_aug[kv][i * w:(i + 2) * w])
        return v_bands[(i, kv)]

    n_h = SWA_HEADS_PER_JOB
    jobs = [(i, h0) for i in range(SWA_BLOCKS_PER_STEP) for h0 in range(0, N_HEADS, n_h)]

    def scores(job):
        i, h0 = job
        qx = jnp.concatenate(
            [q_ref[0, i * w:(i + 1) * w, head * LANES:(head + 1) * LANES]
             for head in range(h0, h0 + n_h)], axis=0)
        return lax.dot_general(qx, k_band(i), (((1,), (1,)), ((), ())),
                               preferred_element_type=F32)

    def absorb(job, s_all):
        i, h0 = job
        kv = h0 // SWA_GROUP
        table = jnp.minimum(step, 1) if i == 0 else 1
        p_rows = []
        for g in range(n_h):
            s = s_all[g * w:(g + 1) * w] + bias_ref[table, h0 + g]
            m = jnp.max(s, axis=-1, keepdims=True)
            p_rows.append(jnp.exp2(s - m).astype(BF16))
        acc = jnp.dot(jnp.concatenate(p_rows, axis=0), v_band(i, kv),
                      preferred_element_type=F32)
        for pair in range(n_h // HEADS_PER_LANE_BLOCK):
            num, den = [], []
            for d in range(HEADS_PER_LANE_BLOCK):
                g = pair * HEADS_PER_LANE_BLOCK + d
                a = acc[g * w:(g + 1) * w]
                num.append(a[:, d * LANES:(d + 1) * LANES])
                den.append(a[:, (1 - d) * LANES:(2 - d) * LANES])
            blk = h0 // HEADS_PER_LANE_BLOCK + pair
            o_ref[0, i * w:(i + 1) * w, blk * LANES:(blk + 1) * LANES] = (
                jnp.where(lane < HEAD_DIM, num[0], num[1])
                / jnp.where(lane < HEAD_DIM, den[0], den[1])).astype(BF16)

    pending = []
    for job in jobs:
        pending.append((job, scores(job)))
        if len(pending) > SWA_SCORES_AHEAD:
            absorb(*pending.pop(0))
    for item in pending:
        absorb(*item)


def _swa_attn(q, k, v, sinks):
    b, s, _ = q.shape
    w = WINDOW
    n = SWA_BLOCKS_PER_STEP
    kv_w = SWA_KV_HEADS * HEAD_DIM
    prev = pl.BlockSpec((1, w, kv_w), lambda i, j, *_: (i, jnp.maximum(j * n - 1, 0), 0))
    cur = pl.BlockSpec((1, n * w, kv_w), lambda i, j, *_: (i, j, 0))
    grid_spec = pltpu.PrefetchScalarGridSpec(
        num_scalar_prefetch=1,
        grid=(b, s // (n * w)),
        in_specs=[pl.BlockSpec((1, n * w, N_HEADS * LANES), lambda i, j, *_: (i, j, 0)),
                  prev, cur, prev, cur],
        out_specs=pl.BlockSpec((1, n * w, D_MODEL), lambda i, j, *_: (i, j, 0)),
        scratch_shapes=[pltpu.VMEM((2, N_HEADS, w, 2 * w), F32)],
    )
    return pl.pallas_call(
        _swa_attn_kernel,
        grid_spec=grid_spec,
        out_shape=jax.ShapeDtypeStruct((b, s, D_MODEL), BF16),
        compiler_params=_params("arbitrary", "arbitrary"),
        name="swa_attn",
    )(sinks.astype(F32), q, k, k, v, v)


def kernel(x, l0_ffn1_norm, l0_ffn1_w_gate, l0_ffn1_w_up, l0_ffn1_w_down, l0_mix_norm, l0_fox_w_in, l0_fox_b_forget, l0_fox_w_out, l0_ffn2_norm, l0_ffn2_w_gate, l0_ffn2_w_up, l0_ffn2_w_down, l1_ffn1_norm, l1_ffn1_w_gate, l1_ffn1_w_up, l1_ffn1_w_down, l1_mix_norm, l1_swa_w_in, l1_swa_sinks, l1_swa_w_out, l1_ffn2_norm, l1_ffn2_w_gate, l1_ffn2_w_up, l1_ffn2_w_down, final_norm):
    b, s, d = x.shape
    t = b * s
    h = x.reshape(t, d)

    def whole(*weights):
        return [(w, w.shape[1]) for w in weights]


    h, (fox_w_qkv, fox_w_out, wg, wu, wd) = _ffn(
        h, l0_ffn1_norm, l0_ffn1_w_gate, l0_ffn1_w_up, l0_ffn1_w_down,
        casts=[(l0_fox_w_in, 3 * d)] + whole(l0_fox_w_out, l0_ffn2_w_gate, l0_ffn2_w_up,
                                             l0_ffn2_w_down))
    q, k, v_t = _fox_proj(h.reshape(b, s, d), l0_mix_norm, l0_fox_w_in, fox_w_qkv,
                          l0_fox_b_forget)
    o = _fox_attn(q, k, v_t)
    h, (wg, wu, wd) = _ffn(h, l0_ffn2_norm, wg, wu, wd, attn=o.reshape(t, d), w_out=fox_w_out,
                           casts=whole(l1_ffn1_w_gate, l1_ffn1_w_up, l1_ffn1_w_down))

    h, (swa_w_in, swa_w_out, wg2, wu2, wd2) = _ffn(
        h, l1_ffn1_norm, wg, wu, wd,
        casts=whole(l1_swa_w_in, l1_swa_w_out, l1_ffn2_w_gate, l1_ffn2_w_up, l1_ffn2_w_down))
    q, k, v = _swa_proj(h, l1_mix_norm, swa_w_in)
    kv_w = SWA_KV_HEADS * HEAD_DIM
    o = _swa_attn(q.reshape(b, s, N_HEADS * LANES), k.reshape(b, s, kv_w),
                  v.reshape(b, s, kv_w), l1_swa_sinks)
    h, _ = _ffn(h, l1_ffn2_norm, wg2, wu2, wd2, attn=o.reshape(t, d), w_out=swa_w_out,
                final_g=final_norm)
    return h.reshape(b, s, d)
```
